```python
import math
import jax, jax.numpy as jnp
from jax import lax
import numpy as np

D_MODEL = 2048
BATCH = 8
SEQ = 4096
DEPTH = 1
DEC_BATCH = 32
DEC_SEQ = 64
PAST_LEN = 1024

CHUNK = 64
Q_BLOCK = 128
H_A = 8
DA = 64
DV_A = 2 * DA
H_B = 8
DB = 128
A_QK = H_A * 2 * DA
A_V = H_A * DV_A
B_QK = H_B * DB
SPLIT_SIZES = (A_QK, A_QK, A_V, B_QK, B_QK, B_QK, H_B, D_MODEL, D_MODEL)
N_IN = 2 * A_QK + A_V + 3 * B_QK + H_B + 2 * D_MODEL
N_EXPERTS = 32
TOP_K = 4
D_FF = D_MODEL
SWIGLU_LIMIT = 7.0
SWIGLU_ALPHA = 1.702
EXPERT_BLOCK = 256
NORM_EPS = 1e-6

kernel_name = "streaming_diffattn_fox_moe_step"


def _rms(x, g):
    xf = x.astype(jnp.float32)
    y = xf * lax.rsqrt(jnp.mean(xf * xf, axis=-1, keepdims=True) + NORM_EPS)
    return (y * g.astype(jnp.float32)).astype(x.dtype)


def _split_offsets():
    offs, acc = [], 0
    for s in SPLIT_SIZES[:-1]:
        acc += s
        offs.append(acc)
    return offs


def _sweep_queries(block_fn, q_pos, q_args):
    sq = q_pos.shape[0]
    if sq > Q_BLOCK and sq % Q_BLOCK == 0:
        n = sq // Q_BLOCK
        pos_b = q_pos.reshape(n, Q_BLOCK)
        args_b = tuple(jnp.moveaxis(a.reshape(a.shape[0], n, Q_BLOCK, *a.shape[2:]), 1, 0) for a in q_args)
        out = lax.map(lambda t: block_fn(t[0], *t[1]), (pos_b, args_b))
        out = jnp.moveaxis(out, 0, 1)
        return out.reshape(out.shape[0], sq, *out.shape[3:])
    return block_fn(q_pos, *q_args)


def _diff_attention(q, k, v, q_pos, k_pos, lam):
    k1, k2 = k[..., :DA], k[..., DA:]
    slopes = jnp.exp2(-8.0 * jnp.arange(1, H_A + 1, dtype=jnp.float32) / H_A)
    scale = DA ** -0.5

    def block(qp, q1, q2):
        allowed = (k_pos[None, :] // CHUNK) <= (qp[:, None] // CHUNK)
        dist = jnp.abs(qp[:, None] - k_pos[None, :]).astype(jnp.float32)
        bias = jnp.where(allowed[None], -slopes[:, None, None] * dist[None], -jnp.inf)
        s1 = jnp.einsum('bqhd,bkhd->bhqk', q1, k1, preferred_element_type=jnp.float32) * scale + bias
        s2 = jnp.einsum('bqhd,bkhd->bhqk', q2, k2, preferred_element_type=jnp.float32) * scale + bias
        p = jax.nn.softmax(s1, axis=-1) - lam * jax.nn.softmax(s2, axis=-1)
        return jnp.einsum('bhqk,bkhd->bqhd', p.astype(v.dtype), v)

    return _sweep_queries(block, q_pos, (q[:, :, :, 0], q[:, :, :, 1]))


def _forgetting_attention(q, k, v, c_q, c_k, q_pos, k_pos):
    scale = DB ** -0.5
    ck_t = jnp.moveaxis(c_k, 1, 2)

    def block(qp, qb, cqb):
        allowed = k_pos[None, :] <= qp[:, None]
        decay = jnp.moveaxis(cqb, 1, 2)[..., None] - ck_t[:, :, None, :]
        s = jnp.einsum('bqhd,bkhd->bhqk', qb, k, preferred_element_type=jnp.float32) * scale + decay
        p = jax.nn.softmax(jnp.where(allowed, s, -jnp.inf), axis=-1)
        return jnp.einsum('bhqk,bkhd->bqhd', p.astype(v.dtype), v)

    return _sweep_queries(block, q_pos, (q, c_q))


def _moe(h, w_router, b_router, w_gate_up, b_gate_up, w_down, b_down):
    bsz, s, d = h.shape
    xt = h.reshape(-1, d)
    m = xt.shape[0]
    logits = (xt @ w_router).astype(jnp.float32) + b_router.astype(jnp.float32)
    top_v, top_e = lax.top_k(logits, TOP_K)
    gate = jax.nn.softmax(top_v, axis=-1)
    a = m * TOP_K
    flat_e = top_e.reshape(-1).astype(jnp.int32)
    flat_t = jnp.repeat(jnp.arange(m, dtype=jnp.int32), TOP_K)
    order = jnp.argsort(flat_e * a + jnp.arange(a, dtype=jnp.int32))
    se, st = flat_e[order], flat_t[order]
    sg = gate.reshape(-1)[order]
    counts = jnp.bincount(flat_e, length=N_EXPERTS).astype(jnp.int32)
    padded = (counts + EXPERT_BLOCK - 1) // EXPERT_BLOCK * EXPERT_BLOCK
    pad_end = jnp.cumsum(padded)
    pad_start = pad_end - padded
    start = jnp.cumsum(counts) - counts
    dest = pad_start[se] + jnp.arange(a, dtype=jnp.int32) - start[se]
    n_blocks = -(-a // EXPERT_BLOCK) + N_EXPERTS
    rows = n_blocks * EXPERT_BLOCK
    row_tok = jnp.full((rows,), m, jnp.int32).at[dest].set(st)
    blk_e = jnp.minimum(jnp.searchsorted(pad_end, jnp.arange(n_blocks, dtype=jnp.int32) * EXPERT_BLOCK, side='right'), N_EXPERTS - 1)
    x_pad = jnp.concatenate([xt, jnp.zeros((1, d), xt.dtype)], axis=0)
    xb = x_pad[row_tok].reshape(n_blocks, EXPERT_BLOCK, d)

    def expert_block(args):
        xe, e = args
        hu = xe @ w_gate_up[e] + b_gate_up[e]
        g, u = hu[..., :D_FF], hu[..., D_FF:]
        g = jnp.minimum(g, SWIGLU_LIMIT)
        u = jnp.clip(u, -SWIGLU_LIMIT, SWIGLU_LIMIT)
        act = (u + 1.0) * g * jax.nn.sigmoid(SWIGLU_ALPHA * g)
        return act @ w_down[e] + b_down[e]

    yb = lax.map(expert_block, (xb, blk_e)).reshape(rows, d)
    contrib = yb[dest] * sg[:, None].astype(yb.dtype)
    out = jax.ops.segment_sum(contrib, st, num_segments=m)
    return out.reshape(bsz, s, d)


def _layer(x, past, lam_init, g_attn, w_in, b_gate, a_q_norm, a_k_norm, b_q_norm, b_k_norm, b_f,
           lambda_q1, lambda_k1, lambda_q2, lambda_k2, a_subln, w_o_a, w_o_b, w_out, g_ffn,
           w_router, b_router, w_gate_up, b_gate_up, w_down, b_down):
    bsz, s = x.shape[:2]
    h = _rms(x, g_attn)
    proj = h @ w_in
    aq, ak, av, bq, bk, bv, bfl, ga, gb = jnp.split(proj, _split_offsets(), axis=-1)
    aq = _rms(aq.reshape(bsz, s, H_A, 2, DA), a_q_norm)
    ak = _rms(ak.reshape(bsz, s, H_A, 2, DA), a_k_norm).reshape(bsz, s, H_A, 2 * DA)
    av = av.reshape(bsz, s, H_A, DV_A)
    bq = _rms(bq.reshape(bsz, s, H_B, DB), b_q_norm)
    bk = _rms(bk.reshape(bsz, s, H_B, DB), b_k_norm)
    bv = bv.reshape(bsz, s, H_B, DB)
    logf = jax.nn.log_sigmoid(bfl.astype(jnp.float32) + b_f.astype(jnp.float32))
    gate_a = jax.nn.sigmoid(ga + b_gate[:D_MODEL])
    gate_b = jax.nn.sigmoid(gb + b_gate[D_MODEL:])

    if past is None:
        past_len = 0
        ka_all, va_all, kb_all, vb_all, logf_all = ak, av, bk, bv, logf
    else:
        pa_k, pa_v, pb_k, pb_v, pb_logf = past
        past_len = pa_k.shape[1]
        ka_all = jnp.concatenate([pa_k, ak], axis=1)
        va_all = jnp.concatenate([pa_v, av], axis=1)
        kb_all = jnp.concatenate([pb_k, bk], axis=1)
        vb_all = jnp.concatenate([pb_v, bv], axis=1)
        logf_all = jnp.concatenate([pb_logf.astype(jnp.float32), logf], axis=1)
    q_pos = past_len + jnp.arange(s, dtype=jnp.int32)
    k_pos = jnp.arange(past_len + s, dtype=jnp.int32)

    lam = (jnp.exp(jnp.sum(lambda_q1.astype(jnp.float32) * lambda_k1.astype(jnp.float32)))
           - jnp.exp(jnp.sum(lambda_q2.astype(jnp.float32) * lambda_k2.astype(jnp.float32))) + lam_init)
    o_a = _diff_attention(aq, ka_all, va_all, q_pos, k_pos, lam)
    o_a = _rms(o_a, a_subln) * (1.0 - lam_init)
    y_a = o_a.reshape(bsz, s, A_V) @ w_o_a

    c = jnp.cumsum(logf_all, axis=1)
    o_b = _forgetting_attention(bq, kb_all, vb_all, c[:, past_len:], c, q_pos, k_pos)
    y_b = o_b.reshape(bsz, s, B_QK) @ w_o_b

    x = x + (gate_a * y_a + gate_b * y_b) @ w_out
    x = x + _moe(_rms(x, g_ffn), w_router, b_router, w_gate_up, b_gate_up, w_down, b_down)
    return x, (ak, av, bk, bv, logf)


def setup_inputs(seed: int = 0) -> dict:
    key = jax.random.key(seed)
    ks = jax.random.split(key, 32)
    nrm = lambda k, shape: jax.random.normal(k, shape, jnp.float32)
    L, D, E, F = DEPTH, D_MODEL, N_EXPERTS, D_FF
    return {
        "x_prompt": nrm(ks[0], (BATCH, SEQ, D)),
        "x_sample": nrm(ks[1], (DEC_BATCH, DEC_SEQ, D)),
        "cache_a_k": nrm(ks[2], (L, DEC_BATCH, PAST_LEN, H_A, 2 * DA)),
        "cache_a_v": nrm(ks[3], (L, DEC_BATCH, PAST_LEN, H_A, DV_A)),
        "cache_b_k": nrm(ks[4], (L, DEC_BATCH, PAST_LEN, H_B, DB)),
        "cache_b_v": nrm(ks[5], (L, DEC_BATCH, PAST_LEN, H_B, DB)),
        "cache_b_logf": jax.nn.log_sigmoid(2.0 + 0.5 * nrm(ks[6], (L, DEC_BATCH, PAST_LEN, H_B))),
        "g_attn": 1.0 + 0.02 * nrm(ks[7], (L, D)),
        "w_in": nrm(ks[8], (L, D, N_IN)) * D ** -0.5,
        "b_gate": 0.1 * nrm(ks[9], (L, 2 * D)),
        "a_q_norm": 1.0 + 0.02 * nrm(ks[10], (L, DA)),
        "a_k_norm": 1.0 + 0.02 * nrm(ks[11], (L, DA)),
        "b_q_norm": 1.0 + 0.02 * nrm(ks[12], (L, DB)),
        "b_k_norm": 1.0 + 0.02 * nrm(ks[13], (L, DB)),
        "b_f": 2.0 + 0.5 * nrm(ks[14], (L, H_B)),
        "lambda_q1": 0.1 * nrm(ks[15], (L, DA)),
        "lambda_k1": 0.1 * nrm(ks[16], (L, DA)),
        "lambda_q2": 0.1 * nrm(ks[17], (L, DA)),
        "lambda_k2": 0.1 * nrm(ks[18], (L, DA)),
        "a_subln": 1.0 + 0.02 * nrm(ks[19], (L, DV_A)),
        "w_o_a": nrm(ks[20], (L, A_V, D)) * A_V ** -0.5,
        "w_o_b": nrm(ks[21], (L, B_QK, D)) * B_QK ** -0.5,
        "w_out": nrm(ks[22], (L, D, D)) * D ** -0.5,
        "g_ffn": 1.0 + 0.02 * nrm(ks[23], (L, D)),
        "w_router": nrm(ks[24], (L, D, E)) * D ** -0.5,
        "b_router": 0.01 * nrm(ks[25], (L, E)),
        "w_gate_up": nrm(ks[26], (L, E, D, 2 * F)) * D ** -0.5,
        "b_gate_up": 0.01 * nrm(ks[27], (L, E, 2 * F)),
        "w_down": nrm(ks[28], (L, E, F, D)) * F ** -0.5,
        "b_down": 0.01 * nrm(ks[29], (L, E, D)),
    }


def reference(x_prompt, x_sample, cache_a_k, cache_a_v, cache_b_k, cache_b_v, cache_b_logf,
              g_attn, w_in, b_gate, a_q_norm, a_k_norm, b_q_norm, b_k_norm, b_f,
              lambda_q1, lambda_k1, lambda_q2, lambda_k2, a_subln, w_o_a, w_o_b, w_out, g_ffn,
              w_router, b_router, w_gate_up, b_gate_up, w_down, b_down):
    yp, ys = x_prompt, x_sample
    new = [[] for _ in range(10)]
    for l in range(DEPTH):
        lw = dict(g_attn=g_attn[l], w_in=w_in[l], b_gate=b_gate[l], a_q_norm=a_q_norm[l], a_k_norm=a_k_norm[l],
                  b_q_norm=b_q_norm[l], b_k_norm=b_k_norm[l], b_f=b_f[l], lambda_q1=lambda_q1[l],
                  lambda_k1=lambda_k1[l], lambda_q2=lambda_q2[l], lambda_k2=lambda_k2[l], a_subln=a_subln[l],
                  w_o_a=w_o_a[l], w_o_b=w_o_b[l], w_out=w_out[l], g_ffn=g_ffn[l], w_router=w_router[l],
                  b_router=b_router[l], w_gate_up=w_gate_up[l], b_gate_up=b_gate_up[l], w_down=w_down[l],
                  b_down=b_down[l])
        lam_init = 0.8 - 0.6 * math.exp(-0.3 * l)
        yp, st_p = _layer(yp, None, lam_init, **lw)
        ys, st_s = _layer(ys, (cache_a_k[l], cache_a_v[l], cache_b_k[l], cache_b_v[l], cache_b_logf[l]), lam_init, **lw)
        for i, arr in enumerate(st_p + st_s):
            new[i].append(arr)
    pak, pav, pbk, pbv, pbf, sak, sav, sbk, sbv, sbf = [jnp.stack(arrs) for arrs in new]
    return (yp, ys, pak, pav, pbk, pbv, pbf, sak, sav, sbk, sbv, sbf)
```

```python
import functools
import math

import jax
import jax.numpy as jnp
from jax import lax
from jax.experimental import pallas as pl
from jax.experimental.pallas import tpu as pltpu

F32 = jnp.float32
BF16 = jnp.bfloat16
I32 = jnp.int32

D_MODEL = 2048
CHUNK = 64
H_A, DA, DV_A = 8, 64, 128
H_B, DB = 8, 128
A_QK, A_V, B_QK = H_A * 2 * DA, H_A * DV_A, H_B * DB
N_EXPERTS, TOP_K, D_FF = 32, 4, 2048
SWIGLU_LIMIT, SWIGLU_ALPHA = 7.0, 1.702
NORM_EPS = 1e-6
LAM_INIT = 0.8 - 0.6 * math.exp(-0.3 * 0)

LANES = 128
VMEM_LIMIT = 56 * 1024 * 1024
NEG = -1e30

TM_NORM = 512
TM_PROJ = 512
BQ_A = 256
BQ_B = 512
TM_MERGE = 512
TN_MERGE = 512
TM_ROUTE = 512
TT_MOVE = 256
TM_EXP = 512
TF_EXP = 512


def _params(n_axes):
    return pltpu.CompilerParams(dimension_semantics=("arbitrary",) * n_axes, vmem_limit_bytes=VMEM_LIMIT)


def _rms_body(x_ref, g_ref, o_ref):
    x = x_ref[...]
    ms = jnp.mean(x * x, axis=-1, keepdims=True)
    o_ref[...] = (x * lax.rsqrt(ms + NORM_EPS) * g_ref[...]).astype(o_ref.dtype)


def _rms_bf16(x, g):
    m, d = x.shape
    tm = min(TM_NORM, m)
    return pl.pallas_call(
        _rms_body,
        grid=(m // tm,),
        in_specs=[pl.BlockSpec((tm, d), lambda i: (i, 0)), pl.BlockSpec((1, d), lambda i: (0, 0))],
        out_specs=pl.BlockSpec((tm, d), lambda i: (i, 0)),
        out_shape=jax.ShapeDtypeStruct((m, d), BF16),
        compiler_params=_params(1),
        name="rms_norm",
    )(x, g.reshape(1, d))


def _proj_plain_body(h_ref, w_ref, o_ref):
    o_ref[...] = jnp.dot(h_ref[...], w_ref[...], preferred_element_type=F32)


def _proj_norm_body(h_ref, w_ref, g_ref, o_ref, *, group, scale):
    y = jnp.dot(h_ref[...], w_ref[...], preferred_element_type=F32)
    lane = lax.broadcasted_iota(I32, (y.shape[0], LANES), 1)
    gain = g_ref[...] * scale
    for s in range(y.shape[1] // LANES):
        slab = y[:, s * LANES:(s + 1) * LANES]
        sq = slab * slab
        if group == LANES:
            inv = lax.rsqrt(jnp.mean(sq, axis=1, keepdims=True) + NORM_EPS)
        else:
            lo = lane < group
            s_lo = jnp.sum(jnp.where(lo, sq, 0.0), axis=1, keepdims=True)
            s_hi = jnp.sum(jnp.where(lo, 0.0, sq), axis=1, keepdims=True)
            inv = jnp.where(lo, lax.rsqrt(s_lo / group + NORM_EPS), lax.rsqrt(s_hi / group + NORM_EPS))
        o_ref[:, s * LANES:(s + 1) * LANES] = (slab * inv * gain).astype(o_ref.dtype)


def _proj_logf_body(h_ref, w_ref, b_ref, o_ref, ot_ref):
    z = jnp.dot(h_ref[...], w_ref[...], preferred_element_type=F32) + b_ref[...]
    logf = jnp.minimum(z, 0.0) - jnp.log1p(jnp.exp(-jnp.abs(z)))
    o_ref[...] = logf[:, :H_B]
    ot_ref[...] = logf.T[:H_B, :]


def _proj(body, h, w, extra, out_shapes, out_cols):
    m, d = h.shape
    n = w.shape[1]
    tm = min(TM_PROJ, m)
    in_specs = [pl.BlockSpec((tm, d), lambda i: (i, 0)), pl.BlockSpec((d, n), lambda i: (0, 0))]
    in_specs += [pl.BlockSpec(e.shape, lambda i: (0, 0)) for e in extra]
    return pl.pallas_call(
        body,
        grid=(m // tm,),
        in_specs=in_specs,
        out_specs=out_cols(tm),
        out_shape=out_shapes,
        compiler_params=_params(1),
        name="proj",
    )(h, w, *extra)


def _project_qkv(h, w_in, a_q_norm, a_k_norm, b_q_norm, b_k_norm, b_f):
    m = h.shape[0]
    w = w_in.astype(BF16)
    o = 0
    cols = {}
    for name, width in (("aq", A_QK), ("ak", A_QK), ("av", A_V), ("bq", B_QK), ("bk", B_QK), ("bv", B_QK)):
        cols[name] = w[:, o:o + width]
        o += width
    w_f = jnp.pad(w[:, o:o + H_B], ((0, 0), (0, LANES - H_B)))
    b_f_pad = jnp.pad(b_f.astype(F32), (0, LANES - H_B)).reshape(1, LANES)

    def tile_spec(n):
        return lambda tm: pl.BlockSpec((tm, n), lambda i: (i, 0))

    def normed(name, gain, group, scale, dtype):
        g = jnp.tile(gain.astype(F32), LANES // group).reshape(1, LANES)
        body = functools.partial(_proj_norm_body, group=group, scale=scale)
        n = cols[name].shape[1]
        return _proj(body, h, cols[name], [g], jax.ShapeDtypeStruct((m, n), dtype), tile_spec(n))

    def plain(name):
        n = cols[name].shape[1]
        return _proj(_proj_plain_body, h, cols[name], [], jax.ShapeDtypeStruct((m, n), F32), tile_spec(n))

    aq = normed("aq", a_q_norm, DA, DA ** -0.5, BF16)
    ak = normed("ak", a_k_norm, DA, 1.0, F32)
    av = plain("av")
    bq = normed("bq", b_q_norm, DB, DB ** -0.5, BF16)
    bk = normed("bk", b_k_norm, DB, 1.0, F32)
    bv = plain("bv")
    logf, logf_t = _proj(
        _proj_logf_body, h, w_f, [b_f_pad],
        (jax.ShapeDtypeStruct((m, H_B), F32), jax.ShapeDtypeStruct((H_B, m), F32)),
        lambda tm: (pl.BlockSpec((tm, H_B), lambda i: (i, 0)), pl.BlockSpec((H_B, tm), lambda i: (0, i))))
    return aq, ak, av, bq, bk, bv, logf, logf_t


def _cumsum_body(x_ref, o_ref):
    x = x_ref[...]
    n = x.shape[1]
    lane = lax.broadcasted_iota(I32, x.shape, 1)
    shift = 1
    while shift < n:
        x = x + jnp.where(lane >= shift, pltpu.roll(x, shift, 1), 0.0)
        shift *= 2
    o_ref[...] = x


def _cumsum_lanes(x):
    b, h, s = x.shape
    return pl.pallas_call(
        _cumsum_body,
        grid=(b,),
        in_specs=[pl.BlockSpec((None, h, s), lambda i: (i, 0, 0))],
        out_specs=pl.BlockSpec((None, h, s), lambda i: (i, 0, 0)),
        out_shape=jax.ShapeDtypeStruct((b, h, s), F32),
        compiler_params=_params(1),
        name="cumsum",
    )(x)


def _nt_dot(a, b):
    return lax.dot_general(a, b, (((1,), (1,)), ((), ())), preferred_element_type=F32)


def _online_softmax_step(s, v, m_scr, l_scr, acc_scr):
    m_prev = m_scr[...]
    m_new = jnp.maximum(m_prev, jnp.max(s, axis=1, keepdims=True))
    alpha = jnp.exp(m_prev - m_new)
    p = jnp.exp(s - m_new)
    l_scr[...] = alpha * l_scr[...] + jnp.sum(p, axis=1, keepdims=True)
    acc_scr[...] = alpha * acc_scr[...] + jnp.dot(p.astype(BF16), v, preferred_element_type=F32)
    m_scr[...] = m_new


def _init_softmax(m_scr, l_scr, acc_scr):
    m_scr[...] = jnp.full(m_scr.shape, NEG, F32)
    l_scr[...] = jnp.zeros(l_scr.shape, F32)
    acc_scr[...] = jnp.zeros(acc_scr.shape, F32)


def _stack_components(q):
    lane = lax.broadcasted_iota(I32, q.shape, 1)
    zero = jnp.zeros_like(q)
    return jnp.concatenate([jnp.where(lane < DA, q, zero), jnp.where(lane >= DA, q, zero)], axis=0)


def _diff_combine(acc, l, lam, gain, bq):
    o = acc[:bq] / l[:bq] - lam * (acc[bq:] / l[bq:])
    inv = lax.rsqrt(jnp.mean(o * o, axis=1, keepdims=True) + NORM_EPS)
    return o * inv * gain * (1.0 - LAM_INIT)


def _attn_a_prompt_body(slope_ref, lam_ref, q_ref, k_ref, v_ref, g_ref, o_ref, m_scr, l_scr, acc_scr, *, bq):
    h = pl.program_id(1)
    qi = pl.program_id(2)
    slope = slope_ref[h]
    qq = _stack_components(q_ref[...])
    _init_softmax(m_scr, l_scr, acc_scr)
    col = lax.broadcasted_iota(I32, (1, bq), 1).astype(F32)

    def past_tile(j, carry):
        k0 = pl.multiple_of(j * bq, bq)
        k = k_ref[pl.ds(k0, bq), :].astype(BF16)
        v = v_ref[pl.ds(k0, bq), :].astype(BF16)
        bias = slope * (col + ((j - qi) * bq).astype(F32))
        _online_softmax_step(_nt_dot(qq, k) + bias, v, m_scr, l_scr, acc_scr)
        return carry

    lax.fori_loop(0, qi, past_tile, 0)

    k0 = pl.multiple_of(qi * bq, bq)
    k = k_ref[pl.ds(k0, bq), :].astype(BF16)
    v = v_ref[pl.ds(k0, bq), :].astype(BF16)
    r = lax.broadcasted_iota(I32, (2 * bq, bq), 0)
    r = jnp.where(r >= bq, r - bq, r)
    c = lax.broadcasted_iota(I32, (2 * bq, bq), 1)
    bias = slope * (r - jnp.abs(r - c)).astype(F32)
    s = jnp.where(c // CHUNK <= r // CHUNK, _nt_dot(qq, k) + bias, NEG)
    _online_softmax_step(s, v, m_scr, l_scr, acc_scr)

    o_ref[...] = _diff_combine(acc_scr[...], l_scr[...], lam_ref[0], g_ref[...], bq).astype(o_ref.dtype)


def _attn_a_prompt(aq, ak, av, slopes, lam, gain, batch, seq):
    bq = min(BQ_A, seq)
    nq = seq // bq
    smem = pl.BlockSpec(memory_space=pltpu.SMEM)
    return pl.pallas_call(
        functools.partial(_attn_a_prompt_body, bq=bq),
        grid=(batch, H_A, nq),
        in_specs=[smem, smem,
                  pl.BlockSpec((bq, LANES), lambda b, h, i: (b * nq + i, h)),
                  pl.BlockSpec((seq, LANES), lambda b, h, i: (b, h)),
                  pl.BlockSpec((seq, LANES), lambda b, h, i: (b, h)),
                  pl.BlockSpec((1, LANES), lambda b, h, i: (0, 0))],
        out_specs=pl.BlockSpec((bq, LANES), lambda b, h, i: (b * nq + i, h)),
        out_shape=jax.ShapeDtypeStruct(aq.shape, BF16),
        scratch_shapes=[pltpu.VMEM((2 * bq, 1), F32), pltpu.VMEM((2 * bq, 1), F32),
                        pltpu.VMEM((2 * bq, LANES), F32)],
        compiler_params=_params(3),
        name="attn_a_prompt",
    )(slopes, lam, aq, ak, av, gain)


def _attn_b_prompt_body(q_ref, k_ref, v_ref, c_ref, o_ref, m_scr, l_scr, acc_scr, *, bq):
    qi = pl.program_id(2)
    q = q_ref[...]
    _init_softmax(m_scr, l_scr, acc_scr)
    c_tile = c_ref[pl.ds(qi, 1), :]
    c_first = c_tile[:, 0:1]

    def past_tile(j, carry):
        k0 = pl.multiple_of(j * bq, bq)
        k = k_ref[pl.ds(k0, bq), :].astype(BF16)
        v = v_ref[pl.ds(k0, bq), :].astype(BF16)
        bias = c_first - c_ref[pl.ds(j, 1), :]
        _online_softmax_step(_nt_dot(q, k) + bias, v, m_scr, l_scr, acc_scr)
        return carry

    lax.fori_loop(0, qi, past_tile, 0)

    k0 = pl.multiple_of(qi * bq, bq)
    k = k_ref[pl.ds(k0, bq), :].astype(BF16)
    v = v_ref[pl.ds(k0, bq), :].astype(BF16)
    r = lax.broadcasted_iota(I32, (bq, bq), 0)
    c = lax.broadcasted_iota(I32, (bq, bq), 1)
    s = jnp.where(c <= r, _nt_dot(q, k) + (c_first - c_tile), NEG)
    _online_softmax_step(s, v, m_scr, l_scr, acc_scr)

    o_ref[...] = (acc_scr[...] / l_scr[...]).astype(o_ref.dtype)


def _attn_b_prompt(bq_arr, bk, bv, c, batch, seq):
    bq = min(BQ_B, seq)
    nq = seq // bq
    c4 = c.reshape(batch, H_B, nq, bq)
    return pl.pallas_call(
        functools.partial(_attn_b_prompt_body, bq=bq),
        grid=(batch, H_B, nq),
        in_specs=[pl.BlockSpec((bq, LANES), lambda b, h, i: (b * nq + i, h)),
                  pl.BlockSpec((seq, LANES), lambda b, h, i: (b, h)),
                  pl.BlockSpec((seq, LANES), lambda b, h, i: (b, h)),
                  pl.BlockSpec((None, None, nq, bq), lambda b, h, i: (b, h, 0, 0))],
        out_specs=pl.BlockSpec((bq, LANES), lambda b, h, i: (b * nq + i, h)),
        out_shape=jax.ShapeDtypeStruct(bq_arr.shape, BF16),
        scratch_shapes=[pltpu.VMEM((bq, 1), F32), pltpu.VMEM((bq, 1), F32), pltpu.VMEM((bq, LANES), F32)],
        compiler_params=_params(3),
        name="attn_b_prompt",
    )(bq_arr, bk, bv, c4)


def _two_part_softmax(s_c, s_n, v_c, v_n):
    m = jnp.maximum(jnp.max(s_c, axis=1, keepdims=True), jnp.max(s_n, axis=1, keepdims=True))
    p_c = jnp.exp(s_c - m)
    p_n = jnp.exp(s_n - m)
    l = jnp.sum(p_c, axis=1, keepdims=True) + jnp.sum(p_n, axis=1, keepdims=True)
    acc = (jnp.dot(p_c.astype(BF16), v_c, preferred_element_type=F32)
           + jnp.dot(p_n.astype(BF16), v_n, preferred_element_type=F32))
    return acc, l


def _attn_a_sample_body(slope_ref, lam_ref, q_ref, kc_ref, vc_ref, kn_ref, vn_ref, g_ref, o_ref, *, past, sq):
    col_c = lax.broadcasted_iota(I32, (1, past), 1).astype(F32) - float(past)
    r = lax.broadcasted_iota(I32, (2 * sq, sq), 0)
    r = jnp.where(r >= sq, r - sq, r)
    c = lax.broadcasted_iota(I32, (2 * sq, sq), 1)
    rel_n = (r - jnp.abs(r - c)).astype(F32)
    for h in range(H_A):
        hs = slice(h * LANES, (h + 1) * LANES)
        slope = slope_ref[h]
        qq = _stack_components(q_ref[:, hs])
        s_c = _nt_dot(qq, kc_ref[:, hs].astype(BF16)) + slope * col_c
        s_n = _nt_dot(qq, kn_ref[:, hs].astype(BF16)) + slope * rel_n
        acc, l = _two_part_softmax(s_c, s_n, vc_ref[:, hs].astype(BF16), vn_ref[:, hs].astype(BF16))
        o_ref[:, hs] = _diff_combine(acc, l, lam_ref[0], g_ref[...], sq).astype(o_ref.dtype)


def _attn_b_sample_body(q_ref, kc_ref, vc_ref, kn_ref, vn_ref, c_ref, o_ref, *, past, sq):
    r = lax.broadcasted_iota(I32, (sq, sq), 0)
    c = lax.broadcasted_iota(I32, (sq, sq), 1)
    causal = c <= r
    for h in range(H_B):
        hs = slice(h * LANES, (h + 1) * LANES)
        q = q_ref[:, hs]
        c_row = c_ref[h:h + 1, :]
        c_first = c_row[:, past:past + 1]
        s_c = _nt_dot(q, kc_ref[:, hs].astype(BF16)) + (c_first - c_row[:, :past])
        s_n = _nt_dot(q, kn_ref[:, hs].astype(BF16)) + (c_first - c_row[:, past:past + sq])
        s_n = jnp.where(causal, s_n, NEG)
        acc, l = _two_part_softmax(s_c, s_n, vc_ref[:, hs].astype(BF16), vn_ref[:, hs].astype(BF16))
        o_ref[:, hs] = (acc / l).astype(o_ref.dtype)


def _attn_sample(body, q, kc, vc, kn, vn, tail, smem_args, batch, past, sq):
    width = q.shape[1]
    smem = pl.BlockSpec(memory_space=pltpu.SMEM)
    in_specs = [smem] * len(smem_args) + [
        pl.BlockSpec((sq, width), lambda b: (b, 0)),
        pl.BlockSpec((None, past, width), lambda b: (b, 0, 0)),
        pl.BlockSpec((None, past, width), lambda b: (b, 0, 0)),
        pl.BlockSpec((sq, width), lambda b: (b, 0)),
        pl.BlockSpec((sq, width), lambda b: (b, 0))]
    if tail.ndim == 3:
        in_specs.append(pl.BlockSpec((None,) + tail.shape[1:], lambda b: (b, 0, 0)))
    else:
        in_specs.append(pl.BlockSpec(tail.shape, lambda b: (0, 0)))
    return pl.pallas_call(
        functools.partial(body, past=past, sq=sq),
        grid=(batch,),
        in_specs=in_specs,
        out_specs=pl.BlockSpec((sq, width), lambda b: (b, 0)),
        out_shape=jax.ShapeDtypeStruct(q.shape, BF16),
        compiler_params=_params(1),
        name="attn_sample",
    )(*smem_args, q, kc, vc, kn, vn, tail)


def _merge_body(x_ref, h_ref, oa_ref, ob_ref, woa_ref, wob_ref, wga_ref, wgb_ref, bga_ref, bgb_ref, wout_ref,
                o_ref, acc_scr):
    j = pl.program_id(1)

    @pl.when(j == 0)
    def _():
        acc_scr[...] = jnp.zeros(acc_scr.shape, F32)

    h = h_ref[...]
    y_a = jnp.dot(oa_ref[...], woa_ref[...], preferred_element_type=F32)
    y_b = jnp.dot(ob_ref[...], wob_ref[...], preferred_element_type=F32)
    g_a = jax.nn.sigmoid(jnp.dot(h, wga_ref[...], preferred_element_type=F32) + bga_ref[...])
    g_b = jax.nn.sigmoid(jnp.dot(h, wgb_ref[...], preferred_element_type=F32) + bgb_ref[...])
    merged = (g_a * y_a + g_b * y_b).astype(BF16)
    acc_scr[...] += jnp.dot(merged, wout_ref[...], preferred_element_type=F32)

    @pl.when(j == pl.num_programs(1) - 1)
    def _():
        o_ref[...] = x_ref[...] + acc_scr[...]


def _merge(x, h, o_a, o_b, w_o_a, w_o_b, w_ga, w_gb, b_ga, b_gb, w_out):
    m, d = x.shape
    tm = min(TM_MERGE, m)
    tn = TN_MERGE
    row = lambda n: pl.BlockSpec((tm, n), lambda i, j: (i, 0))
    colw = lambda k: pl.BlockSpec((k, tn), lambda i, j: (0, j))
    return pl.pallas_call(
        _merge_body,
        grid=(m // tm, d // tn),
        in_specs=[row(d), row(d), row(A_V), row(B_QK), colw(A_V), colw(B_QK), colw(d), colw(d),
                  pl.BlockSpec((1, tn), lambda i, j: (0, j)), pl.BlockSpec((1, tn), lambda i, j: (0, j)),
                  pl.BlockSpec((tn, d), lambda i, j: (j, 0))],
        out_specs=row(d),
        out_shape=jax.ShapeDtypeStruct((m, d), F32),
        scratch_shapes=[pltpu.VMEM((tm, d), F32)],
        compiler_params=_params(2),
        name="merge",
    )(x, h, o_a, o_b, w_o_a, w_o_b, w_ga, w_gb, b_ga, b_gb, w_out)


def _router_body(x_ref, g_ref, whi_ref, wlo_ref, b_ref, cin_ref, sel_ref, gate_ref, cout_ref, cnt_scr):
    i = pl.program_id(0)

    @pl.when(i == 0)
    def _():
        cnt_scr[...] = cin_ref[...]

    x = x_ref[...]
    tm = x.shape[0]
    h = x * lax.rsqrt(jnp.mean(x * x, axis=-1, keepdims=True) + NORM_EPS) * g_ref[...]
    hi = h.astype(BF16)
    lo = (h - hi.astype(F32)).astype(BF16)
    logits = (jnp.dot(hi, whi_ref[...], preferred_element_type=F32)
              + jnp.dot(hi, wlo_ref[...], preferred_element_type=F32)
              + jnp.dot(lo, whi_ref[...], preferred_element_type=F32)) + b_ref[...]

    lane = lax.broadcasted_iota(I32, logits.shape, 1)
    work = logits
    picks, vals, ids = [], [], []
    for _ in range(TOP_K):
        mx = jnp.max(work, axis=1, keepdims=True)
        idx = jnp.min(jnp.where(work == mx, lane, LANES), axis=1, keepdims=True)
        pick = lane == idx
        work = jnp.where(pick, -jnp.inf, work)
        picks.append(pick)
        vals.append(mx)
        ids.append(idx)

    exps = [jnp.exp(v - vals[0]) for v in vals]
    denom = exps[0] + exps[1] + exps[2] + exps[3]

    picked = jnp.zeros(logits.shape, F32)
    for pick in picks:
        picked = picked + jnp.where(pick, 1.0, 0.0)
    r = lax.broadcasted_iota(I32, (tm, tm), 0)
    c = lax.broadcasted_iota(I32, (tm, tm), 1)
    earlier = jnp.where(c < r, 1.0, 0.0).astype(BF16)
    before = jnp.dot(earlier, picked.astype(BF16), preferred_element_type=F32) + cnt_scr[...]

    sel = jnp.zeros(logits.shape, I32)
    gate = jnp.zeros(logits.shape, F32)
    for k in range(TOP_K):
        rank = jnp.sum(jnp.where(picks[k], before, 0.0), axis=1, keepdims=True).astype(I32)
        sel = jnp.where(lane == k, ids[k], sel)
        sel = jnp.where(lane == TOP_K + k, rank, sel)
        gate = jnp.where(lane == k, exps[k] / denom, gate)
    sel_ref[...] = sel
    gate_ref[...] = gate
    cnt_scr[...] += jnp.sum(picked, axis=0, keepdims=True)
    cout_ref[...] = cnt_scr[...]


def _router(x, g, w_hi, w_lo, b, counts_in):
    m, d = x.shape
    tm = min(TM_ROUTE, m)
    const = lambda shape: pl.BlockSpec(shape, lambda i: (0, 0))
    return pl.pallas_call(
        _router_body,
        grid=(m // tm,),
        in_specs=[pl.BlockSpec((tm, d), lambda i: (i, 0)), const((1, d)), const((d, LANES)), const((d, LANES)),
                  const((1, LANES)), const((1, LANES))],
        out_specs=(pl.BlockSpec((tm, LANES), lambda i: (i, 0)), pl.BlockSpec((tm, LANES), lambda i: (i, 0)),
                   const((1, LANES))),
        out_shape=(jax.ShapeDtypeStruct((m, LANES), I32), jax.ShapeDtypeStruct((m, LANES), F32),
                   jax.ShapeDtypeStruct((1, LANES), F32)),
        scratch_shapes=[pltpu.VMEM((1, LANES), F32)],
        compiler_params=_params(1),
        name="router",
    )(x, g, w_hi, w_lo, b, counts_in)


def _dispatch_body(fill_ref, dest_ref, xp_ref, xs_ref, o_ref, zero_scr, sem, fill_sem, *, tt, tiles_p, tm,
                   n_blocks):
    i = pl.program_id(0)

    @pl.when(i == 0)
    def _():
        zero_scr[...] = jnp.zeros(zero_scr.shape, F32)
        for e in range(N_EXPERTS):
            @pl.when(fill_ref[e] >= 0)
            def _():
                off = pl.multiple_of(fill_ref[e], tm)
                pltpu.make_async_copy(zero_scr, o_ref.at[pl.ds(off, tm)], fill_sem).start()
        for e in range(N_EXPERTS):
            @pl.when(fill_ref[e] >= 0)
            def _():
                pltpu.make_async_copy(zero_scr, o_ref.at[pl.ds(0, tm)], fill_sem).wait()

        def zero_block(b, carry):
            off = pl.multiple_of(b * tm, tm)
            pltpu.make_async_copy(zero_scr, o_ref.at[pl.ds(off, tm)], fill_sem).start()
            return carry

        def wait_block(b, carry):
            pltpu.make_async_copy(zero_scr, o_ref.at[pl.ds(0, tm)], fill_sem).wait()
            return carry

        lax.fori_loop(fill_ref[N_EXPERTS], n_blocks, zero_block, 0)
        lax.fori_loop(fill_ref[N_EXPERTS], n_blocks, wait_block, 0)

    def move(src_ref, base):
        def issue(t, carry):
            for k in range(TOP_K):
                d = dest_ref[t * TOP_K + k]
                pltpu.make_async_copy(src_ref.at[pl.ds(base + t, 1)], o_ref.at[pl.ds(d, 1)], sem).start()
            return carry

        lax.fori_loop(0, tt, issue, 0)

        def drain(t, carry):
            for k in range(TOP_K):
                pltpu.make_async_copy(src_ref.at[pl.ds(0, 1)], o_ref.at[pl.ds(0, 1)], sem).wait()
            return carry

        lax.fori_loop(0, tt, drain, 0)

    @pl.when(i < tiles_p)
    def _():
        move(xp_ref, i * tt)

    @pl.when(i >= tiles_p)
    def _():
        move(xs_ref, (i - tiles_p) * tt)


def _dispatch(fill, dest_flat, x_p, x_s, rows, tm):
    m_p, d = x_p.shape
    m_s = x_s.shape[0]
    tt = TT_MOVE
    tiles_p = m_p // tt
    tiles = tiles_p + m_s // tt
    any_spec = pl.BlockSpec(memory_space=pl.ANY)
    return pl.pallas_call(
        functools.partial(_dispatch_body, tt=tt, tiles_p=tiles_p, tm=tm, n_blocks=rows // tm),
        grid=(tiles,),
        in_specs=[pl.BlockSpec(memory_space=pltpu.SMEM),
                  pl.BlockSpec((tt * TOP_K,), lambda i: (i,), memory_space=pltpu.SMEM),
                  any_spec, any_spec],
        out_specs=any_spec,
        out_shape=jax.ShapeDtypeStruct((rows, d), F32),
        scratch_shapes=[pltpu.VMEM((tm, d), F32), pltpu.SemaphoreType.DMA, pltpu.SemaphoreType.DMA],
        compiler_params=pltpu.CompilerParams(dimension_semantics=("arbitrary",), vmem_limit_bytes=VMEM_LIMIT,
                                             has_side_effects=True),
        name="dispatch",
    )(fill, dest_flat, x_p, x_s)


def _experts_body(blk_ref, used_ref, x_ref, g_ref, wg_ref, wu_ref, bg_ref, bu_ref, wd_ref, bd_ref, o_ref,
                  h_scr, acc_scr):
    b = pl.program_id(0)
    f = pl.program_id(1)

    @pl.when(b < used_ref[0])
    def _():
        @pl.when(f == 0)
        def _():
            x = x_ref[...]
            inv = lax.rsqrt(jnp.mean(x * x, axis=-1, keepdims=True) + NORM_EPS)
            h_scr[...] = (x * inv * g_ref[...]).astype(BF16)
            acc_scr[...] = jnp.zeros(acc_scr.shape, F32)

        h = h_scr[...]
        g = jnp.dot(h, wg_ref[...], preferred_element_type=F32) + bg_ref[...]
        u = jnp.dot(h, wu_ref[...], preferred_element_type=F32) + bu_ref[...]
        g = jnp.minimum(g, SWIGLU_LIMIT)
        u = jnp.clip(u, -SWIGLU_LIMIT, SWIGLU_LIMIT)
        act = (u + 1.0) * g * jax.nn.sigmoid(SWIGLU_ALPHA * g)
        acc_scr[...] += jnp.dot(act.astype(BF16), wd_ref[...], preferred_element_type=F32)

        @pl.when(f == pl.num_programs(1) - 1)
        def _():
            o_ref[...] = acc_scr[...] + bd_ref[...]

    @pl.when((b >= used_ref[0]) & (f == 0))
    def _():
        o_ref[...] = jnp.zeros(o_ref.shape, F32)


def _experts(blk_e, n_used, xs, g_ffn, w_gu, b_gu, w_d, b_d, tm):
    rows, d = xs.shape
    tf = TF_EXP
    nf = D_FF // tf
    n_blocks = rows // tm

    def blk(b, used):
        return jnp.minimum(b, used[0] - 1)

    def fch(b, f, used):
        return jnp.where(b < used[0], f, nf - 1)

    grid_spec = pltpu.PrefetchScalarGridSpec(
        num_scalar_prefetch=2,
        grid=(n_blocks, nf),
        in_specs=[
            pl.BlockSpec((tm, d), lambda b, f, e, u: (blk(b, u), 0)),
            pl.BlockSpec((1, d), lambda b, f, e, u: (0, 0)),
            pl.BlockSpec((None, d, tf), lambda b, f, e, u: (e[blk(b, u)], 0, fch(b, f, u))),
            pl.BlockSpec((None, d, tf), lambda b, f, e, u: (e[blk(b, u)], 0, nf + fch(b, f, u))),
            pl.BlockSpec((None, 1, tf), lambda b, f, e, u: (e[blk(b, u)], 0, fch(b, f, u))),
            pl.BlockSpec((None, 1, tf), lambda b, f, e, u: (e[blk(b, u)], 0, nf + fch(b, f, u))),
            pl.BlockSpec((None, tf, d), lambda b, f, e, u: (e[blk(b, u)], fch(b, f, u), 0)),
            pl.BlockSpec((None, 1, d), lambda b, f, e, u: (e[blk(b, u)], 0, 0)),
        ],
        out_specs=pl.BlockSpec((tm, d), lambda b, f, e, u: (b, 0)),
        scratch_shapes=[pltpu.VMEM((tm, d), BF16), pltpu.VMEM((tm, d), F32)],
    )
    return pl.pallas_call(
        _experts_body,
        grid_spec=grid_spec,
        out_shape=jax.ShapeDtypeStruct((rows, d), F32),
        compiler_params=_params(2),
        name="experts",
    )(blk_e, n_used, xs, g_ffn, w_gu, w_gu, b_gu, b_gu, w_d, b_d)


def _combine_body(dest_ref, x_ref, gate_ref, ys_ref, o_ref, buf, sem, *, tt):
    def issue(t, carry):
        for k in range(TOP_K):
            d = dest_ref[t * TOP_K + k]
            pltpu.make_async_copy(ys_ref.at[pl.ds(d, 1)], buf.at[k, pl.ds(t, 1)], sem).start()
        return carry

    lax.fori_loop(0, tt, issue, 0)

    def drain(t, carry):
        for k in range(TOP_K):
            pltpu.make_async_copy(ys_ref.at[pl.ds(0, 1)], buf.at[k, pl.ds(0, 1)], sem).wait()
        return carry

    lax.fori_loop(0, tt, drain, 0)

    gate = gate_ref[...]
    out = x_ref[...]
    for k in range(TOP_K):
        out = out + gate[:, k:k + 1] * buf[k]
    o_ref[...] = out


def _combine(dest_flat, x, gate, ys):
    m, d = x.shape
    tt = TT_MOVE
    return pl.pallas_call(
        functools.partial(_combine_body, tt=tt),
        grid=(m // tt,),
        in_specs=[pl.BlockSpec((tt * TOP_K,), lambda i: (i,), memory_space=pltpu.SMEM),
                  pl.BlockSpec((tt, d), lambda i: (i, 0)),
                  pl.BlockSpec((tt, LANES), lambda i: (i, 0)),
                  pl.BlockSpec(memory_space=pl.ANY)],
        out_specs=pl.BlockSpec((tt, d), lambda i: (i, 0)),
        out_shape=jax.ShapeDtypeStruct((m, d), F32),
        scratch_shapes=[pltpu.VMEM((TOP_K, tt, d), F32), pltpu.SemaphoreType.DMA],
        compiler_params=_params(1),
        name="combine",
    )(dest_flat, x, gate, ys)


def _moe(x_p, x_s, g_ffn, w_router, b_router, w_gate_up, b_gate_up, w_down, b_down):
    d = x_p.shape[1]
    tm = TM_EXP
    g = g_ffn.astype(F32).reshape(1, d)
    w_r = jnp.pad(w_router.astype(F32), ((0, 0), (0, LANES - N_EXPERTS)))
    w_hi = w_r.astype(BF16)
    w_lo = (w_r - w_hi.astype(F32)).astype(BF16)
    b_r = jnp.pad(b_router.astype(F32), (0, LANES - N_EXPERTS), constant_values=NEG).reshape(1, LANES)

    sel_p, gate_p, counts = _router(x_p, g, w_hi, w_lo, b_r, jnp.zeros((1, LANES), F32))
    sel_s, gate_s, counts = _router(x_s, g, w_hi, w_lo, b_r, counts)

    counts = counts[0, :N_EXPERTS].astype(I32)
    padded = (counts + tm - 1) // tm * tm
    pad_end = jnp.cumsum(padded)
    pad_start = pad_end - padded
    n_assign = (x_p.shape[0] + x_s.shape[0]) * TOP_K
    n_blocks = -(-n_assign // tm) + N_EXPERTS
    rows = n_blocks * tm
    blk_e = jnp.minimum(jnp.searchsorted(pad_end, jnp.arange(n_blocks, dtype=I32) * tm, side="right"),
                        N_EXPERTS - 1).astype(I32)
    n_used = (pad_end[-1:] // tm).astype(I32)
    fill = jnp.concatenate([jnp.where(counts > 0, pad_end - tm, -1), n_used]).astype(I32)

    def dest_of(sel):
        return (pad_start[sel[:, :TOP_K]] + sel[:, TOP_K:2 * TOP_K]).reshape(-1)

    dest_p, dest_s = dest_of(sel_p), dest_of(sel_s)
    xs = _dispatch(fill, jnp.concatenate([dest_p, dest_s]), x_p, x_s, rows, tm)
    ys = _experts(blk_e, n_used, xs, g, w_gate_up.astype(BF16), b_gate_up.astype(F32)[:, None, :],
                  w_down.astype(BF16), b_down.astype(F32)[:, None, :], tm)
    return _combine(dest_p, x_p, gate_p, ys), _combine(dest_s, x_s, gate_s, ys)


def kernel(x_prompt, x_sample, cache_a_k, cache_a_v, cache_b_k, cache_b_v, cache_b_logf, g_attn, w_in, b_gate,
           a_q_norm, a_k_norm, b_q_norm, b_k_norm, b_f, lambda_q1, lambda_k1, lambda_q2, lambda_k2, a_subln,
           w_o_a, w_o_b, w_out, g_ffn, w_router, b_router, w_gate_up, b_gate_up, w_down, b_down):
    depth = g_attn.shape[0]
    assert depth == 1, "single-layer trunk"
    batch, seq, d = x_prompt.shape
    dec_batch, dec_seq, _ = x_sample.shape
    past = cache_a_k.shape[2]
    assert dec_seq == CHUNK and past % CHUNK == 0, "the new sample frames must form exactly one chunk"
    sk = past + dec_seq
    sk_pad = -(-sk // LANES) * LANES

    xp = x_prompt.reshape(batch * seq, d)
    xs = x_sample.reshape(dec_batch * dec_seq, d)
    g_a = g_attn[0].astype(F32)
    w_in0 = w_in[0]
    split = 2 * A_QK + A_V + 3 * B_QK + H_B
    w_ga = w_in0[:, split:split + d].astype(BF16)
    w_gb = w_in0[:, split + d:split + 2 * d].astype(BF16)
    b_ga = b_gate[0, :d].astype(F32).reshape(1, d)
    b_gb = b_gate[0, d:].astype(F32).reshape(1, d)
    w_oa, w_ob, w_o = w_o_a[0].astype(BF16), w_o_b[0].astype(BF16), w_out[0].astype(BF16)

    slopes = jnp.exp2(-8.0 * jnp.arange(1, H_A + 1, dtype=F32) / H_A)
    lam = (jnp.exp(jnp.sum(lambda_q1[0].astype(F32) * lambda_k1[0].astype(F32)))
           - jnp.exp(jnp.sum(lambda_q2[0].astype(F32) * lambda_k2[0].astype(F32))) + LAM_INIT).reshape(1)
    subln = a_subln[0].astype(F32).reshape(1, DV_A)

    norms = (a_q_norm[0], a_k_norm[0], b_q_norm[0], b_k_norm[0], b_f[0])

    hp = _rms_bf16(xp, g_a)
    aq, pak, pav, bq, pbk, pbv, plogf, plogf_t = _project_qkv(hp, w_in0, *norms)
    cp = _cumsum_lanes(plogf_t.reshape(H_B, batch, seq).transpose(1, 0, 2))
    oa_p = _attn_a_prompt(aq, pak, pav, slopes, lam, subln, batch, seq)
    ob_p = _attn_b_prompt(bq, pbk, pbv, cp, batch, seq)
    x2_p = _merge(xp, hp, oa_p, ob_p, w_oa, w_ob, w_ga, w_gb, b_ga, b_gb, w_o)

    hs = _rms_bf16(xs, g_a)
    aq, sak, sav, bq, sbk, sbv, slogf, slogf_t = _project_qkv(hs, w_in0, *norms)
    logf_all = jnp.concatenate(
        [cache_b_logf[0].astype(F32).transpose(0, 2, 1),
         slogf_t.reshape(H_B, dec_batch, dec_seq).transpose(1, 0, 2),
         jnp.zeros((dec_batch, H_B, sk_pad - sk), F32)], axis=2)
    cs = _cumsum_lanes(logf_all)
    oa_s = _attn_sample(_attn_a_sample_body, aq, cache_a_k[0].reshape(dec_batch, past, A_QK),
                        cache_a_v[0].reshape(dec_batch, past, A_V), sak, sav, subln, (slopes, lam),
                        dec_batch, past, dec_seq)
    ob_s = _attn_sample(_attn_b_sample_body, bq, cache_b_k[0].reshape(dec_batch, past, B_QK),
                        cache_b_v[0].reshape(dec_batch, past, B_QK), sbk, sbv, cs, (), dec_batch, past, dec_seq)
    x2_s = _merge(xs, hs, oa_s, ob_s, w_oa, w_ob, w_ga, w_gb, b_ga, b_gb, w_o)

    y_p, y_s = _moe(x2_p, x2_s, g_ffn[0], w_router[0], b_router[0], w_gate_up[0], b_gate_up[0], w_down[0],
                    b_down[0])

    def heads(a, b, s, h):
        return a.reshape(1, b, s, h, -1)

    return (y_p.reshape(batch, seq, d), y_s.reshape(dec_batch, dec_seq, d),
            heads(pak, batch, seq, H_A), heads(pav, batch, seq, H_A), heads(pbk, batch, seq, H_B),
            heads(pbv, batch, seq, H_B), plogf.reshape(1, batch, seq, H_B),
            heads(sak, dec_batch, dec_seq, H_A), heads(sav, dec_batch, dec_seq, H_A),
            heads(sbk, dec_batch, dec_seq, H_B), heads(sbv, dec_batch, dec_seq, H_B),
            slogf.reshape(1, dec_batch, dec_seq, H_B))
```

```python
import functools
import math

import jax
import jax.numpy as jnp
from jax import lax
from jax.experimental import pallas as pl
from jax.experimental.pallas import tpu as pltpu

F32 = jnp.float32
BF16 = jnp.bfloat16
I32 = jnp.int32

D_MODEL = 2048
CHUNK = 64
H_A, DA, DV_A = 8, 64, 128
H_B, DB = 8, 128
A_QK, A_V, B_QK = H_A * 2 * DA, H_A * DV_A, H_B * DB
N_EXPERTS, TOP_K, D_FF = 32, 4, 2048
SWIGLU_LIMIT, SWIGLU_ALPHA = 7.0, 1.702
NORM_EPS = 1e-6
LAM_INIT = 0.8 - 0.6 * math.exp(-0.3 * 0)

LANES = 128
VMEM_LIMIT = 56 * 1024 * 1024
NEG = -1e30

TM_NORM = 512
TM_PROJ = 512
BQ_A = 512
BQ_B = 512
TM_MERGE = 512
TN_MERGE = 512
TM_ROUTE = 512
TT_MOVE = 256
TM_EXP = 512
TF_EXP = 512


def _params(n_axes):
    return pltpu.CompilerParams(dimension_semantics=("arbitrary",) * n_axes, vmem_limit_bytes=VMEM_LIMIT)


def _rms_body(x_ref, g_ref, o_ref):
    x = x_ref[...]
    ms = jnp.mean(x * x, axis=-1, keepdims=True)
    o_ref[...] = (x * lax.rsqrt(ms + NORM_EPS) * g_ref[...]).astype(o_ref.dtype)


def _rms_bf16(x, g):
    m, d = x.shape
    tm = min(TM_NORM, m)
    return pl.pallas_call(
        _rms_body,
        grid=(m // tm,),
        in_specs=[pl.BlockSpec((tm, d), lambda i: (i, 0)), pl.BlockSpec((1, d), lambda i: (0, 0))],
        out_specs=pl.BlockSpec((tm, d), lambda i: (i, 0)),
        out_shape=jax.ShapeDtypeStruct((m, d), BF16),
        compiler_params=_params(1),
        name="rms_norm",
    )(x, g.reshape(1, d))


def _store_all(o_refs, cols, val):
    for o_ref in o_refs:
        o_ref[:, cols] = val.astype(o_ref.dtype)


def _proj_plain_body(h_ref, w_ref, *o_refs):
    _store_all(o_refs, slice(None), jnp.dot(h_ref[...], w_ref[...], preferred_element_type=F32))


def _proj_norm_body(h_ref, w_ref, g_ref, *o_refs, group, scale):
    y = jnp.dot(h_ref[...], w_ref[...], preferred_element_type=F32)
    lane = lax.broadcasted_iota(I32, (y.shape[0], LANES), 1)
    gain = g_ref[...] * scale
    for s in range(y.shape[1] // LANES):
        cols = slice(s * LANES, (s + 1) * LANES)
        slab = y[:, cols]
        sq = slab * slab
        if group == LANES:
            inv = lax.rsqrt(jnp.mean(sq, axis=1, keepdims=True) + NORM_EPS)
        else:
            lo = lane < group
            s_lo = jnp.sum(jnp.where(lo, sq, 0.0), axis=1, keepdims=True)
            s_hi = jnp.sum(jnp.where(lo, 0.0, sq), axis=1, keepdims=True)
            inv = jnp.where(lo, lax.rsqrt(s_lo / group + NORM_EPS), lax.rsqrt(s_hi / group + NORM_EPS))
        _store_all(o_refs, cols, slab * inv * gain)


def _proj_logf_body(h_ref, w_ref, b_ref, o_ref, ot_ref):
    z = jnp.dot(h_ref[...], w_ref[...], preferred_element_type=F32) + b_ref[...]
    logf = jnp.minimum(z, 0.0) - jnp.log1p(jnp.exp(-jnp.abs(z)))
    o_ref[...] = logf[:, :H_B]
    ot_ref[...] = logf.T[:H_B, :]


def _proj(body, h, w, extra, out_shapes, out_specs):
    m, d = h.shape
    n = w.shape[1]
    tm = min(TM_PROJ, m)
    in_specs = [pl.BlockSpec((tm, d), lambda i: (i, 0)), pl.BlockSpec((d, n), lambda i: (0, 0))]
    in_specs += [pl.BlockSpec(e.shape, lambda i: (0, 0)) for e in extra]
    return pl.pallas_call(
        body,
        grid=(m // tm,),
        in_specs=in_specs,
        out_specs=out_specs(tm),
        out_shape=out_shapes,
        compiler_params=_params(1),
        name="proj",
    )(h, w, *extra)


def _project_qkv(h, w_in, a_q_norm, a_k_norm, b_q_norm, b_k_norm, b_f):
    m = h.shape[0]
    w = w_in.astype(BF16)
    o = 0
    cols = {}
    for name, width in (("aq", A_QK), ("ak", A_QK), ("av", A_V), ("bq", B_QK), ("bk", B_QK), ("bv", B_QK)):
        cols[name] = w[:, o:o + width]
        o += width
    w_f = jnp.pad(w[:, o:o + H_B], ((0, 0), (0, LANES - H_B)))
    b_f_pad = jnp.pad(b_f.astype(F32), (0, LANES - H_B)).reshape(1, LANES)

    def run(body, name, extra, dtypes):
        n = cols[name].shape[1]
        shapes = tuple(jax.ShapeDtypeStruct((m, n), dt) for dt in dtypes)
        specs = lambda tm: tuple(pl.BlockSpec((tm, n), lambda i: (i, 0)) for _ in dtypes)
        return _proj(body, h, cols[name], extra, shapes, specs)

    def normed(name, gain, group, scale, dtypes):
        g = jnp.tile(gain.astype(F32), LANES // group).reshape(1, LANES)
        return run(functools.partial(_proj_norm_body, group=group, scale=scale), name, [g], dtypes)

    (aq,) = normed("aq", a_q_norm, DA, DA ** -0.5, (BF16,))
    ak = normed("ak", a_k_norm, DA, 1.0, (F32, BF16))
    av = run(_proj_plain_body, "av", [], (F32, BF16))
    (bq,) = normed("bq", b_q_norm, DB, DB ** -0.5, (BF16,))
    bk = normed("bk", b_k_norm, DB, 1.0, (F32, BF16))
    bv = run(_proj_plain_body, "bv", [], (F32, BF16))
    logf, logf_t = _proj(
        _proj_logf_body, h, w_f, [b_f_pad],
        (jax.ShapeDtypeStruct((m, H_B), F32), jax.ShapeDtypeStruct((H_B, m), F32)),
        lambda tm: (pl.BlockSpec((tm, H_B), lambda i: (i, 0)), pl.BlockSpec((H_B, tm), lambda i: (0, i))))
    return aq, ak, av, bq, bk, bv, logf, logf_t


def _cumsum_body(x_ref, o_ref):
    x = x_ref[...]
    n = x.shape[1]
    lane = lax.broadcasted_iota(I32, x.shape, 1)
    shift = 1
    while shift < n:
        x = x + jnp.where(lane >= shift, pltpu.roll(x, shift, 1), 0.0)
        shift *= 2
    o_ref[...] = x


def _cumsum_lanes(x):
    b, h, s = x.shape
    return pl.pallas_call(
        _cumsum_body,
        grid=(b,),
        in_specs=[pl.BlockSpec((None, h, s), lambda i: (i, 0, 0))],
        out_specs=pl.BlockSpec((None, h, s), lambda i: (i, 0, 0)),
        out_shape=jax.ShapeDtypeStruct((b, h, s), F32),
        compiler_params=_params(1),
        name="cumsum",
    )(x)


def _nt_dot(a, b):
    return lax.dot_general(a, b, (((1,), (1,)), ((), ())), preferred_element_type=F32)


def _with_ones(v):
    return jnp.concatenate([v, jnp.ones_like(v)], axis=1)


def _online_softmax_step(s, v_aug, m_scr, acc_scr):
    m_prev = m_scr[...]
    m_new = jnp.maximum(m_prev, jnp.max(s, axis=1, keepdims=True))
    alpha = jnp.exp(m_prev - m_new)
    p = jnp.exp(s - jnp.tile(m_new, (1, s.shape[1] // LANES)))
    acc_scr[...] = (jnp.tile(alpha, (1, 2)) * acc_scr[...]
                    + jnp.dot(p.astype(BF16), v_aug, preferred_element_type=F32))
    m_scr[...] = m_new


def _init_softmax(m_scr, acc_scr):
    m_scr[...] = jnp.full(m_scr.shape, NEG, F32)
    acc_scr[...] = jnp.zeros(acc_scr.shape, F32)


def _stack_components(q):
    lane = lax.broadcasted_iota(I32, q.shape, 1)
    zero = jnp.zeros_like(q)
    return jnp.concatenate([jnp.where(lane < DA, q, zero), jnp.where(lane >= DA, q, zero)], axis=0)


def _diff_combine(o1, o2, lam, gain):
    o = o1 - lam * o2
    inv = lax.rsqrt(jnp.mean(o * o, axis=1, keepdims=True) + NORM_EPS)
    return o * inv * gain * (1.0 - LAM_INIT)


def _attn_a_prompt_body(slope_ref, lam_ref, q_ref, k_ref, v_ref, g_ref, o_ref, m_scr, acc_scr, *, bq):
    h = pl.program_id(1)
    qi = pl.program_id(2)
    slope = slope_ref[h]
    qq = _stack_components(q_ref[...])
    _init_softmax(m_scr, acc_scr)
    col = lax.broadcasted_iota(I32, (1, bq), 1).astype(F32)

    def past_tile(j, carry):
        k0 = pl.multiple_of(j * bq, bq)
        bias = slope * (col + ((j - qi) * bq).astype(F32))
        _online_softmax_step(_nt_dot(qq, k_ref[pl.ds(k0, bq), :]) + bias, _with_ones(v_ref[pl.ds(k0, bq), :]),
                             m_scr, acc_scr)
        return carry

    lax.fori_loop(0, qi, past_tile, 0)

    k0 = pl.multiple_of(qi * bq, bq)
    r = lax.broadcasted_iota(I32, (2 * bq, bq), 0)
    r = jnp.where(r >= bq, r - bq, r)
    c = lax.broadcasted_iota(I32, (2 * bq, bq), 1)
    bias = slope * (r - jnp.abs(r - c)).astype(F32)
    s = jnp.where(c // CHUNK <= r // CHUNK, _nt_dot(qq, k_ref[pl.ds(k0, bq), :]) + bias, NEG)
    _online_softmax_step(s, _with_ones(v_ref[pl.ds(k0, bq), :]), m_scr, acc_scr)

    acc = acc_scr[...]
    o1 = acc[:bq, :LANES] / acc[:bq, LANES:]
    o2 = acc[bq:, :LANES] / acc[bq:, LANES:]
    o_ref[...] = _diff_combine(o1, o2, lam_ref[0], g_ref[...]).astype(o_ref.dtype)


def _attn_a_prompt(aq, ak, av, slopes, lam, gain, batch, seq):
    bq = min(BQ_A, seq)
    nq = seq // bq
    smem = pl.BlockSpec(memory_space=pltpu.SMEM)
    return pl.pallas_call(
        functools.partial(_attn_a_prompt_body, bq=bq),
        grid=(batch, H_A, nq),
        in_specs=[smem, smem,
                  pl.BlockSpec((bq, LANES), lambda b, h, i: (b * nq + i, h)),
                  pl.BlockSpec((seq, LANES), lambda b, h, i: (b, h)),
                  pl.BlockSpec((seq, LANES), lambda b, h, i: (b, h)),
                  pl.BlockSpec((1, LANES), lambda b, h, i: (0, 0))],
        out_specs=pl.BlockSpec((bq, LANES), lambda b, h, i: (b * nq + i, h)),
        out_shape=jax.ShapeDtypeStruct(aq.shape, BF16),
        scratch_shapes=[pltpu.VMEM((2 * bq, LANES), F32), pltpu.VMEM((2 * bq, 2 * LANES), F32)],
        compiler_params=_params(3),
        name="attn_a_prompt",
    )(slopes, lam, aq, ak, av, gain)


def _attn_b_prompt_body(q_ref, k_ref, v_ref, c_ref, o_ref, m_scr, acc_scr, *, bq):
    qi = pl.program_id(2)
    q = q_ref[...]
    _init_softmax(m_scr, acc_scr)
    c_tile = c_ref[pl.ds(qi, 1), :]
    c_first = c_tile[:, 0:1]

    def past_tile(j, carry):
        k0 = pl.multiple_of(j * bq, bq)
        bias = c_first - c_ref[pl.ds(j, 1), :]
        _online_softmax_step(_nt_dot(q, k_ref[pl.ds(k0, bq), :]) + bias, _with_ones(v_ref[pl.ds(k0, bq), :]),
                             m_scr, acc_scr)
        return carry

    lax.fori_loop(0, qi, past_tile, 0)

    k0 = pl.multiple_of(qi * bq, bq)
    r = lax.broadcasted_iota(I32, (bq, bq), 0)
    c = lax.broadcasted_iota(I32, (bq, bq), 1)
    s = jnp.where(c <= r, _nt_dot(q, k_ref[pl.ds(k0, bq), :]) + (c_first - c_tile), NEG)
    _online_softmax_step(s, _with_ones(v_ref[pl.ds(k0, bq), :]), m_scr, acc_scr)

    acc = acc_scr[...]
    o_ref[...] = (acc[:, :LANES] / acc[:, LANES:]).astype(o_ref.dtype)


def _attn_b_prompt(bq_arr, bk, bv, c, batch, seq):
    bq = min(BQ_B, seq)
    nq = seq // bq
    c4 = c.reshape(batch, H_B, nq, bq)
    return pl.pallas_call(
        functools.partial(_attn_b_prompt_body, bq=bq),
        grid=(batch, H_B, nq),
        in_specs=[pl.BlockSpec((bq, LANES), lambda b, h, i: (b * nq + i, h)),
                  pl.BlockSpec((seq, LANES), lambda b, h, i: (b, h)),
                  pl.BlockSpec((seq, LANES), lambda b, h, i: (b, h)),
                  pl.BlockSpec((None, None, nq, bq), lambda b, h, i: (b, h, 0, 0))],
        out_specs=pl.BlockSpec((bq, LANES), lambda b, h, i: (b * nq + i, h)),
        out_shape=jax.ShapeDtypeStruct(bq_arr.shape, BF16),
        scratch_shapes=[pltpu.VMEM((bq, LANES), F32), pltpu.VMEM((bq, 2 * LANES), F32)],
        compiler_params=_params(3),
        name="attn_b_prompt",
    )(bq_arr, bk, bv, c4)


def _two_part_softmax(s_c, s_n, v_c, v_n):
    m = jnp.maximum(jnp.max(s_c, axis=1, keepdims=True), jnp.max(s_n, axis=1, keepdims=True))
    p_c = jnp.exp(s_c - m)
    p_n = jnp.exp(s_n - m)
    l = jnp.sum(p_c, axis=1, keepdims=True) + jnp.sum(p_n, axis=1, keepdims=True)
    acc = (jnp.dot(p_c.astype(BF16), v_c, preferred_element_type=F32)
           + jnp.dot(p_n.astype(BF16), v_n, preferred_element_type=F32))
    return acc / l


def _attn_a_sample_body(slope_ref, lam_ref, q_ref, kc_ref, vc_ref, kn_ref, vn_ref, g_ref, o_ref, *, past, sq):
    col_c = lax.broadcasted_iota(I32, (1, past), 1).astype(F32) - float(past)
    r = lax.broadcasted_iota(I32, (2 * sq, sq), 0)
    r = jnp.where(r >= sq, r - sq, r)
    c = lax.broadcasted_iota(I32, (2 * sq, sq), 1)
    rel_n = (r - jnp.abs(r - c)).astype(F32)
    for h in range(H_A):
        hs = slice(h * LANES, (h + 1) * LANES)
        slope = slope_ref[h]
        qq = _stack_components(q_ref[:, hs])
        s_c = _nt_dot(qq, kc_ref[:, h, :].astype(BF16)) + slope * col_c
        s_n = _nt_dot(qq, kn_ref[:, hs]) + slope * rel_n
        o = _two_part_softmax(s_c, s_n, vc_ref[:, h, :].astype(BF16), vn_ref[:, hs])
        o_ref[:, hs] = _diff_combine(o[:sq], o[sq:], lam_ref[0], g_ref[...]).astype(o_ref.dtype)


def _attn_b_sample_body(q_ref, kc_ref, vc_ref, kn_ref, vn_ref, c_ref, o_ref, *, past, sq):
    r = lax.broadcasted_iota(I32, (sq, sq), 0)
    c = lax.broadcasted_iota(I32, (sq, sq), 1)
    causal = c <= r
    for h in range(H_B):
        hs = slice(h * LANES, (h + 1) * LANES)
        q = q_ref[:, hs]
        c_row = c_ref[h:h + 1, :]
        c_first = c_row[:, past:past + 1]
        s_c = _nt_dot(q, kc_ref[:, h, :].astype(BF16)) + (c_first - c_row[:, :past])
        s_n = _nt_dot(q, kn_ref[:, hs]) + (c_first - c_row[:, past:past + sq])
        s_n = jnp.where(causal, s_n, NEG)
        o = _two_part_softmax(s_c, s_n, vc_ref[:, h, :].astype(BF16), vn_ref[:, hs])
        o_ref[:, hs] = o.astype(o_ref.dtype)


def _attn_sample(body, q, kc, vc, kn, vn, tail, smem_args, batch, past, sq):
    width = q.shape[1]
    smem = pl.BlockSpec(memory_space=pltpu.SMEM)
    cache = pl.BlockSpec((None,) + kc.shape[1:], lambda b: (b, 0, 0, 0))
    in_specs = [smem] * len(smem_args) + [
        pl.BlockSpec((sq, width), lambda b: (b, 0)), cache, cache,
        pl.BlockSpec((sq, width), lambda b: (b, 0)),
        pl.BlockSpec((sq, width), lambda b: (b, 0))]
    if tail.ndim == 3:
        in_specs.append(pl.BlockSpec((None,) + tail.shape[1:], lambda b: (b, 0, 0)))
    else:
        in_specs.append(pl.BlockSpec(tail.shape, lambda b: (0, 0)))
    return pl.pallas_call(
        functools.partial(body, past=past, sq=sq),
        grid=(batch,),
        in_specs=in_specs,
        out_specs=pl.BlockSpec((sq, width), lambda b: (b, 0)),
        out_shape=jax.ShapeDtypeStruct(q.shape, BF16),
        compiler_params=_params(1),
        name="attn_sample",
    )(*smem_args, q, kc, vc, kn, vn, tail)


def _merge_body(x_ref, h_ref, oa_ref, ob_ref, woa_ref, wob_ref, wga_ref, wgb_ref, bga_ref, bgb_ref, wout_ref,
                o_ref, acc_scr):
    j = pl.program_id(1)

    @pl.when(j == 0)
    def _():
        acc_scr[...] = jnp.zeros(acc_scr.shape, F32)

    h = h_ref[...]
    y_a = jnp.dot(oa_ref[...], woa_ref[...], preferred_element_type=F32)
    y_b = jnp.dot(ob_ref[...], wob_ref[...], preferred_element_type=F32)
    g_a = jax.nn.sigmoid(jnp.dot(h, wga_ref[...], preferred_element_type=F32) + bga_ref[...])
    g_b = jax.nn.sigmoid(jnp.dot(h, wgb_ref[...], preferred_element_type=F32) + bgb_ref[...])
    merged = (g_a * y_a + g_b * y_b).astype(BF16)
    acc_scr[...] += jnp.dot(merged, wout_ref[...], preferred_element_type=F32)

    @pl.when(j == pl.num_programs(1) - 1)
    def _():
        o_ref[...] = x_ref[...] + acc_scr[...]


def _merge(x, h, o_a, o_b, w_o_a, w_o_b, w_ga, w_gb, b_ga, b_gb, w_out):
    m, d = x.shape
    tm = min(TM_MERGE, m)
    tn = TN_MERGE
    row = lambda n: pl.BlockSpec((tm, n), lambda i, j: (i, 0))
    colw = lambda k: pl.BlockSpec((k, tn), lambda i, j: (0, j))
    return pl.pallas_call(
        _merge_body,
        grid=(m // tm, d // tn),
        in_specs=[row(d), row(d), row(A_V), row(B_QK), colw(A_V), colw(B_QK), colw(d), colw(d),
                  pl.BlockSpec((1, tn), lambda i, j: (0, j)), pl.BlockSpec((1, tn), lambda i, j: (0, j)),
                  pl.BlockSpec((tn, d), lambda i, j: (j, 0))],
        out_specs=row(d),
        out_shape=jax.ShapeDtypeStruct((m, d), F32),
        scratch_shapes=[pltpu.VMEM((tm, d), F32)],
        compiler_params=_params(2),
        name="merge",
    )(x, h, o_a, o_b, w_o_a, w_o_b, w_ga, w_gb, b_ga, b_gb, w_out)


def _router_body(x_ref, g_ref, whi_ref, wlo_ref, b_ref, cin_ref, sel_ref, gate_ref, cout_ref, cnt_scr):
    i = pl.program_id(0)

    @pl.when(i == 0)
    def _():
        cnt_scr[...] = cin_ref[...]

    x = x_ref[...]
    tm = x.shape[0]
    h = x * lax.rsqrt(jnp.mean(x * x, axis=-1, keepdims=True) + NORM_EPS) * g_ref[...]
    hi = h.astype(BF16)
    lo = (h - hi.astype(F32)).astype(BF16)
    logits = (jnp.dot(hi, whi_ref[...], preferred_element_type=F32)
              + jnp.dot(hi, wlo_ref[...], preferred_element_type=F32)
              + jnp.dot(lo, whi_ref[...], preferred_element_type=F32)) + b_ref[...]

    lane = lax.broadcasted_iota(I32, logits.shape, 1)
    work = logits
    picks, vals, ids = [], [], []
    for _ in range(TOP_K):
        mx = jnp.max(work, axis=1, keepdims=True)
        idx = jnp.min(jnp.where(work == mx, lane, LANES), axis=1, keepdims=True)
        pick = lane == idx
        work = jnp.where(pick, -jnp.inf, work)
        picks.append(pick)
        vals.append(mx)
        ids.append(idx)

    exps = [jnp.exp(v - vals[0]) for v in vals]
    denom = exps[0] + exps[1] + exps[2] + exps[3]

    picked = jnp.zeros(logits.shape, F32)
    for pick in picks:
        picked = picked + jnp.where(pick, 1.0, 0.0)
    r = lax.broadcasted_iota(I32, (tm, tm), 0)
    c = lax.broadcasted_iota(I32, (tm, tm), 1)
    earlier = jnp.where(c < r, 1.0, 0.0).astype(BF16)
    before = jnp.dot(earlier, picked.astype(BF16), preferred_element_type=F32) + cnt_scr[...]

    sel = jnp.zeros(logits.shape, I32)
    gate = jnp.zeros(logits.shape, F32)
    for k in range(TOP_K):
        rank = jnp.sum(jnp.where(picks[k], before, 0.0), axis=1, keepdims=True).astype(I32)
        sel = jnp.where(lane == k, ids[k], sel)
        sel = jnp.where(lane == TOP_K + k, rank, sel)
        gate = jnp.where(lane == k, exps[k] / denom, gate)
    sel_ref[...] = sel
    gate_ref[...] = gate
    cnt_scr[...] += jnp.sum(picked, axis=0, keepdims=True)
    cout_ref[...] = cnt_scr[...]


def _router(x, g, w_hi, w_lo, b, counts_in):
    m, d = x.shape
    tm = min(TM_ROUTE, m)
    const = lambda shape: pl.BlockSpec(shape, lambda i: (0, 0))
    return pl.pallas_call(
        _router_body,
        grid=(m // tm,),
        in_specs=[pl.BlockSpec((tm, d), lambda i: (i, 0)), const((1, d)), const((d, LANES)), const((d, LANES)),
                  const((1, LANES)), const((1, LANES))],
        out_specs=(pl.BlockSpec((tm, LANES), lambda i: (i, 0)), pl.BlockSpec((tm, LANES), lambda i: (i, 0)),
                   const((1, LANES))),
        out_shape=(jax.ShapeDtypeStruct((m, LANES), I32), jax.ShapeDtypeStruct((m, LANES), F32),
                   jax.ShapeDtypeStruct((1, LANES), F32)),
        scratch_shapes=[pltpu.VMEM((1, LANES), F32)],
        compiler_params=_params(1),
        name="router",
    )(x, g, w_hi, w_lo, b, counts_in)


def _dispatch_body(fill_ref, dest_ref, xp_ref, xs_ref, o_ref, zero_scr, sem, fill_sem, *, tt, tiles_p, tm,
                   n_blocks):
    i = pl.program_id(0)

    @pl.when(i == 0)
    def _():
        zero_scr[...] = jnp.zeros(zero_scr.shape, F32)
        for e in range(N_EXPERTS):
            @pl.when(fill_ref[e] >= 0)
            def _():
                off = pl.multiple_of(fill_ref[e], tm)
                pltpu.make_async_copy(zero_scr, o_ref.at[pl.ds(off, tm)], fill_sem).start()
        for e in range(N_EXPERTS):
            @pl.when(fill_ref[e] >= 0)
            def _():
                pltpu.make_async_copy(zero_scr, o_ref.at[pl.ds(0, tm)], fill_sem).wait()

        def zero_block(b, carry):
            off = pl.multiple_of(b * tm, tm)
            pltpu.make_async_copy(zero_scr, o_ref.at[pl.ds(off, tm)], fill_sem).start()
            return carry

        def wait_block(b, carry):
            pltpu.make_async_copy(zero_scr, o_ref.at[pl.ds(0, tm)], fill_sem).wait()
            return carry

        lax.fori_loop(fill_ref[N_EXPERTS], n_blocks, zero_block, 0)
        lax.fori_loop(fill_ref[N_EXPERTS], n_blocks, wait_block, 0)

    def move(src_ref):
        def issue(t, carry):
            for k in range(TOP_K):
                d = dest_ref[t * TOP_K + k]
                pltpu.make_async_copy(src_ref.at[pl.ds(t, 1)], o_ref.at[pl.ds(d, 1)], sem).start()
            return carry

        lax.fori_loop(0, tt, issue, 0)

        def drain(t, carry):
            for k in range(TOP_K):
                pltpu.make_async_copy(src_ref.at[pl.ds(0, 1)], o_ref.at[pl.ds(0, 1)], sem).wait()
            return carry

        lax.fori_loop(0, tt, drain, 0)

    @pl.when(i < tiles_p)
    def _():
        move(xp_ref)

    @pl.when(i >= tiles_p)
    def _():
        move(xs_ref)


def _dispatch(fill, dest_flat, x_p, x_s, rows, tm):
    m_p, d = x_p.shape
    m_s = x_s.shape[0]
    tt = TT_MOVE
    tiles_p = m_p // tt
    tiles = tiles_p + m_s // tt
    return pl.pallas_call(
        functools.partial(_dispatch_body, tt=tt, tiles_p=tiles_p, tm=tm, n_blocks=rows // tm),
        grid=(tiles,),
        in_specs=[pl.BlockSpec(memory_space=pltpu.SMEM),
                  pl.BlockSpec((tt * TOP_K,), lambda i: (i,), memory_space=pltpu.SMEM),
                  pl.BlockSpec((tt, d), lambda i: (jnp.minimum(i, tiles_p - 1), 0)),
                  pl.BlockSpec((tt, d), lambda i: (jnp.maximum(i - tiles_p, 0), 0))],
        out_specs=pl.BlockSpec(memory_space=pl.ANY),
        out_shape=jax.ShapeDtypeStruct((rows, d), F32),
        scratch_shapes=[pltpu.VMEM((tm, d), F32), pltpu.SemaphoreType.DMA, pltpu.SemaphoreType.DMA],
        compiler_params=pltpu.CompilerParams(dimension_semantics=("arbitrary",), vmem_limit_bytes=VMEM_LIMIT,
                                             has_side_effects=True),
        name="dispatch",
    )(fill, dest_flat, x_p, x_s)


def _experts_body(blk_ref, used_ref, x_ref, g_ref, wg_ref, wu_ref, bg_ref, bu_ref, wd_ref, bd_ref, o_ref,
                  h_scr, acc_scr):
    b = pl.program_id(0)
    f = pl.program_id(1)

    @pl.when(b < used_ref[0])
    def _():
        @pl.when(f == 0)
        def _():
            x = x_ref[...]
            inv = lax.rsqrt(jnp.mean(x * x, axis=-1, keepdims=True) + NORM_EPS)
            h_scr[...] = (x * inv * g_ref[...]).astype(BF16)
            acc_scr[...] = jnp.zeros(acc_scr.shape, F32)

        h = h_scr[...]
        g = jnp.dot(h, wg_ref[...], preferred_element_type=F32) + bg_ref[...]
        u = jnp.dot(h, wu_ref[...], preferred_element_type=F32) + bu_ref[...]
        g = jnp.minimum(g, SWIGLU_LIMIT)
        u = jnp.clip(u, -SWIGLU_LIMIT, SWIGLU_LIMIT)
        act = (u + 1.0) * g * jax.nn.sigmoid(SWIGLU_ALPHA * g)
        acc_scr[...] += jnp.dot(act.astype(BF16), wd_ref[...], preferred_element_type=F32)

        @pl.when(f == pl.num_programs(1) - 1)
        def _():
            o_ref[...] = acc_scr[...] + bd_ref[...]

    @pl.when((b >= used_ref[0]) & (f == 0))
    def _():
        o_ref[...] = jnp.zeros(o_ref.shape, F32)


def _experts(blk_e, n_used, xs, g_ffn, w_gu, b_gu, w_d, b_d, tm):
    rows, d = xs.shape
    tf = TF_EXP
    nf = D_FF // tf
    n_blocks = rows // tm

    def blk(b, used):
        return jnp.minimum(b, used[0] - 1)

    def fch(b, f, used):
        return jnp.where(b < used[0], f, nf - 1)

    grid_spec = pltpu.PrefetchScalarGridSpec(
        num_scalar_prefetch=2,
        grid=(n_blocks, nf),
        in_specs=[
            pl.BlockSpec((tm, d), lambda b, f, e, u: (blk(b, u), 0)),
            pl.BlockSpec((1, d), lambda b, f, e, u: (0, 0)),
            pl.BlockSpec((None, d, tf), lambda b, f, e, u: (e[blk(b, u)], 0, fch(b, f, u))),
            pl.BlockSpec((None, d, tf), lambda b, f, e, u: (e[blk(b, u)], 0, nf + fch(b, f, u))),
            pl.BlockSpec((None, 1, tf), lambda b, f, e, u: (e[blk(b, u)], 0, fch(b, f, u))),
            pl.BlockSpec((None, 1, tf), lambda b, f, e, u: (e[blk(b, u)], 0, nf + fch(b, f, u))),
            pl.BlockSpec((None, tf, d), lambda b, f, e, u: (e[blk(b, u)], fch(b, f, u), 0)),
            pl.BlockSpec((None, 1, d), lambda b, f, e, u: (e[blk(b, u)], 0, 0)),
        ],
        out_specs=pl.BlockSpec((tm, d), lambda b, f, e, u: (b, 0)),
        scratch_shapes=[pltpu.VMEM((tm, d), BF16), pltpu.VMEM((tm, d), F32)],
    )
    return pl.pallas_call(
        _experts_body,
        grid_spec=grid_spec,
        out_shape=jax.ShapeDtypeStruct((rows, d), F32),
        compiler_params=_params(2),
        name="experts",
    )(blk_e, n_used, xs, g_ffn, w_gu, w_gu, b_gu, b_gu, w_d, b_d)


def _combine_body(dest_ref, x_ref, gate_ref, ys_ref, o_ref, buf, sem, *, tt):
    def issue(t, carry):
        for k in range(TOP_K):
            d = dest_ref[t * TOP_K + k]
            pltpu.make_async_copy(ys_ref.at[pl.ds(d, 1)], buf.at[k, pl.ds(t, 1)], sem).start()
        return carry

    lax.fori_loop(0, tt, issue, 0)

    def drain(t, carry):
        for k in range(TOP_K):
            pltpu.make_async_copy(ys_ref.at[pl.ds(0, 1)], buf.at[k, pl.ds(0, 1)], sem).wait()
        return carry

    lax.fori_loop(0, tt, drain, 0)

    gate = gate_ref[...]
    out = x_ref[...]
    for k in range(TOP_K):
        out = out + gate[:, k:k + 1] * buf[k]
    o_ref[...] = out


def _combine(dest_flat, x, gate, ys):
    m, d = x.shape
    tt = TT_MOVE
    return pl.pallas_call(
        functools.partial(_combine_body, tt=tt),
        grid=(m // tt,),
        in_specs=[pl.BlockSpec((tt * TOP_K,), lambda i: (i,), memory_space=pltpu.SMEM),
                  pl.BlockSpec((tt, d), lambda i: (i, 0)),
                  pl.BlockSpec((tt, LANES), lambda i: (i, 0)),
                  pl.BlockSpec(memory_space=pl.ANY)],
        out_specs=pl.BlockSpec((tt, d), lambda i: (i, 0)),
        out_shape=jax.ShapeDtypeStruct((m, d), F32),
        scratch_shapes=[pltpu.VMEM((TOP_K, tt, d), F32), pltpu.SemaphoreType.DMA],
        compiler_params=_params(1),
        name="combine",
    )(dest_flat, x, gate, ys)


def _moe(x_p, x_s, g_ffn, w_router, b_router, w_gate_up, b_gate_up, w_down, b_down):
    d = x_p.shape[1]
    tm = TM_EXP
    g = g_ffn.astype(F32).reshape(1, d)
    w_r = jnp.pad(w_router.astype(F32), ((0, 0), (0, LANES - N_EXPERTS)))
    w_hi = w_r.astype(BF16)
    w_lo = (w_r - w_hi.astype(F32)).astype(BF16)
    b_r = jnp.pad(b_router.astype(F32), (0, LANES - N_EXPERTS), constant_values=NEG).reshape(1, LANES)

    sel_p, gate_p, counts = _router(x_p, g, w_hi, w_lo, b_r, jnp.zeros((1, LANES), F32))
    sel_s, gate_s, counts = _router(x_s, g, w_hi, w_lo, b_r, counts)

    counts = counts[0, :N_EXPERTS].astype(I32)
    padded = (counts + tm - 1) // tm * tm
    pad_end = jnp.cumsum(padded)
    pad_start = pad_end - padded
    n_assign = (x_p.shape[0] + x_s.shape[0]) * TOP_K
    n_blocks = -(-n_assign // tm) + N_EXPERTS
    rows = n_blocks * tm
    blk_e = jnp.minimum(jnp.searchsorted(pad_end, jnp.arange(n_blocks, dtype=I32) * tm, side="right"),
                        N_EXPERTS - 1).astype(I32)
    n_used = (pad_end[-1:] // tm).astype(I32)
    fill = jnp.concatenate([jnp.where(counts > 0, pad_end - tm, -1), n_used]).astype(I32)

    def dest_of(sel):
        return (pad_start[sel[:, :TOP_K]] + sel[:, TOP_K:2 * TOP_K]).reshape(-1)

    dest_p, dest_s = dest_of(sel_p), dest_of(sel_s)
    xs = _dispatch(fill, jnp.concatenate([dest_p, dest_s]), x_p, x_s, rows, tm)
    ys = _experts(blk_e, n_used, xs, g, w_gate_up.astype(BF16), b_gate_up.astype(F32)[:, None, :],
                  w_down.astype(BF16), b_down.astype(F32)[:, None, :], tm)
    return _combine(dest_p, x_p, gate_p, ys), _combine(dest_s, x_s, gate_s, ys)


def kernel(x_prompt, x_sample, cache_a_k, cache_a_v, cache_b_k, cache_b_v, cache_b_logf, g_attn, w_in, b_gate,
           a_q_norm, a_k_norm, b_q_norm, b_k_norm, b_f, lambda_q1, lambda_k1, lambda_q2, lambda_k2, a_subln,
           w_o_a, w_o_b, w_out, g_ffn, w_router, b_router, w_gate_up, b_gate_up, w_down, b_down):
    depth = g_attn.shape[0]
    assert depth == 1, "single-layer trunk"
    batch, seq, d = x_prompt.shape
    dec_batch, dec_seq, _ = x_sample.shape
    past = cache_a_k.shape[2]
    assert dec_seq == CHUNK and past % CHUNK == 0, "the new sample frames must form exactly one chunk"
    sk = past + dec_seq
    sk_pad = -(-sk // LANES) * LANES

    xp = x_prompt.reshape(batch * seq, d)
    xs = x_sample.reshape(dec_batch * dec_seq, d)
    g_a = g_attn[0].astype(F32)
    w_in0 = w_in[0]
    split = 2 * A_QK + A_V + 3 * B_QK + H_B
    w_ga = w_in0[:, split:split + d].astype(BF16)
    w_gb = w_in0[:, split + d:split + 2 * d].astype(BF16)
    b_ga = b_gate[0, :d].astype(F32).reshape(1, d)
    b_gb = b_gate[0, d:].astype(F32).reshape(1, d)
    w_oa, w_ob, w_o = w_o_a[0].astype(BF16), w_o_b[0].astype(BF16), w_out[0].astype(BF16)

    slopes = jnp.exp2(-8.0 * jnp.arange(1, H_A + 1, dtype=F32) / H_A)
    lam = (jnp.exp(jnp.sum(lambda_q1[0].astype(F32) * lambda_k1[0].astype(F32)))
           - jnp.exp(jnp.sum(lambda_q2[0].astype(F32) * lambda_k2[0].astype(F32))) + LAM_INIT).reshape(1)
    subln = a_subln[0].astype(F32).reshape(1, DV_A)

    norms = (a_q_norm[0], a_k_norm[0], b_q_norm[0], b_k_norm[0], b_f[0])

    hp = _rms_bf16(xp, g_a)
    aq, (pak, ak16), (pav, av16), bq, (pbk, bk16), (pbv, bv16), plogf, plogf_t = _project_qkv(hp, w_in0, *norms)
    cp = _cumsum_lanes(plogf_t.reshape(H_B, batch, seq).transpose(1, 0, 2))
    oa_p = _attn_a_prompt(aq, ak16, av16, slopes, lam, subln, batch, seq)
    ob_p = _attn_b_prompt(bq, bk16, bv16, cp, batch, seq)
    x2_p = _merge(xp, hp, oa_p, ob_p, w_oa, w_ob, w_ga, w_gb, b_ga, b_gb, w_o)

    hs = _rms_bf16(xs, g_a)
    aq, (sak, ak16), (sav, av16), bq, (sbk, bk16), (sbv, bv16), slogf, slogf_t = _project_qkv(hs, w_in0, *norms)
    logf_all = jnp.concatenate(
        [cache_b_logf[0].astype(F32).transpose(0, 2, 1),
         slogf_t.reshape(H_B, dec_batch, dec_seq).transpose(1, 0, 2),
         jnp.zeros((dec_batch, H_B, sk_pad - sk), F32)], axis=2)
    cs = _cumsum_lanes(logf_all)
    oa_s = _attn_sample(_attn_a_sample_body, aq, cache_a_k[0], cache_a_v[0], ak16, av16, subln, (slopes, lam),
                        dec_batch, past, dec_seq)
    ob_s = _attn_sample(_attn_b_sample_body, bq, cache_b_k[0], cache_b_v[0], bk16, bv16, cs, (),
                        dec_batch, past, dec_seq)
    x2_s = _merge(xs, hs, oa_s, ob_s, w_oa, w_ob, w_ga, w_gb, b_ga, b_gb, w_o)

    y_p, y_s = _moe(x2_p, x2_s, g_ffn[0], w_router[0], b_router[0], w_gate_up[0], b_gate_up[0], w_down[0],
                    b_down[0])

    def heads(a, b, s, h):
        return a.reshape(1, b, s, h, -1)

    return (y_p.reshape(batch, seq, d), y_s.reshape(dec_batch, dec_seq, d),
            heads(pak, batch, seq, H_A), heads(pav, batch, seq, H_A), heads(pbk, batch, seq, H_B),
            heads(pbv, batch, seq, H_B), plogf.reshape(1, batch, seq, H_B),
            heads(sak, dec_batch, dec_seq, H_A), heads(sav, dec_batch, dec_seq, H_A),
            heads(sbk, dec_batch, dec_seq, H_B), heads(sbv, dec_batch, dec_seq, H_B),
            slogf.reshape(1, dec_batch, dec_seq, H_B))
```

```python
import functools
import math

import jax
import jax.numpy as jnp
from jax import lax
from jax.experimental import pallas as pl
from jax.experimental.pallas import tpu as pltpu

F32 = jnp.float32
BF16 = jnp.bfloat16
I32 = jnp.int32

D_MODEL = 2048
CHUNK = 64
H_A, DA, DV_A = 8, 64, 128
H_B, DB = 8, 128
A_QK, A_V, B_QK = H_A * 2 * DA, H_A * DV_A, H_B * DB
N_EXPERTS, TOP_K, D_FF = 32, 4, 2048
SWIGLU_LIMIT, SWIGLU_ALPHA = 7.0, 1.702
NORM_EPS = 1e-6
LAM_INIT = 0.8 - 0.6 * math.exp(-0.3 * 0)

LANES = 128
VMEM_LIMIT = 56 * 1024 * 1024
NEG = -1e30

TM_NORM = 512
TM_PROJ = 512
BQ_A = 512
BQ_B = 512
TM_MERGE = 512
TN_MERGE = 512
TM_ROUTE = 512
TT_MOVE = 256
TM_EXP = 512
TF_EXP = 512


def _params(n_axes):
    return pltpu.CompilerParams(dimension_semantics=("arbitrary",) * n_axes, vmem_limit_bytes=VMEM_LIMIT)


def _rms_body(x_ref, g_ref, o_ref):
    x = x_ref[...]
    ms = jnp.mean(x * x, axis=-1, keepdims=True)
    o_ref[...] = (x * lax.rsqrt(ms + NORM_EPS) * g_ref[...]).astype(o_ref.dtype)


def _rms_bf16(x, g):
    m, d = x.shape
    tm = min(TM_NORM, m)
    return pl.pallas_call(
        _rms_body,
        grid=(m // tm,),
        in_specs=[pl.BlockSpec((tm, d), lambda i: (i, 0)), pl.BlockSpec((1, d), lambda i: (0, 0))],
        out_specs=pl.BlockSpec((tm, d), lambda i: (i, 0)),
        out_shape=jax.ShapeDtypeStruct((m, d), BF16),
        compiler_params=_params(1),
        name="rms_norm",
    )(x, g.reshape(1, d))


def _store_all(o_refs, cols, val):
    for o_ref in o_refs:
        o_ref[:, cols] = val.astype(o_ref.dtype)


def _proj_plain_body(h_ref, w_ref, *o_refs):
    _store_all(o_refs, slice(None), jnp.dot(h_ref[...], w_ref[...], preferred_element_type=F32))


def _proj_norm_body(h_ref, w_ref, g_ref, *o_refs, group, scale):
    y = jnp.dot(h_ref[...], w_ref[...], preferred_element_type=F32)
    lane = lax.broadcasted_iota(I32, (y.shape[0], LANES), 1)
    gain = g_ref[...] * scale
    for s in range(y.shape[1] // LANES):
        cols = slice(s * LANES, (s + 1) * LANES)
        slab = y[:, cols]
        sq = slab * slab
        if group == LANES:
            inv = lax.rsqrt(jnp.mean(sq, axis=1, keepdims=True) + NORM_EPS)
        else:
            lo = lane < group
            s_lo = jnp.sum(jnp.where(lo, sq, 0.0), axis=1, keepdims=True)
            s_hi = jnp.sum(jnp.where(lo, 0.0, sq), axis=1, keepdims=True)
            inv = jnp.where(lo, lax.rsqrt(s_lo / group + NORM_EPS), lax.rsqrt(s_hi / group + NORM_EPS))
        _store_all(o_refs, cols, slab * inv * gain)


def _proj_logf_body(h_ref, w_ref, b_ref, o_ref, ot_ref):
    z = jnp.dot(h_ref[...], w_ref[...], preferred_element_type=F32) + b_ref[...]
    logf = jnp.minimum(z, 0.0) - jnp.log1p(jnp.exp(-jnp.abs(z)))
    o_ref[...] = logf[:, :H_B]
    ot_ref[...] = logf.T[:H_B, :]


def _proj(body, h, w, extra, out_shapes, out_specs):
    m, d = h.shape
    n = w.shape[1]
    tm = min(TM_PROJ, m)
    in_specs = [pl.BlockSpec((tm, d), lambda i: (i, 0)), pl.BlockSpec((d, n), lambda i: (0, 0))]
    in_specs += [pl.BlockSpec(e.shape, lambda i: (0, 0)) for e in extra]
    return pl.pallas_call(
        body,
        grid=(m // tm,),
        in_specs=in_specs,
        out_specs=out_specs(tm),
        out_shape=out_shapes,
        compiler_params=_params(1),
        name="proj",
    )(h, w, *extra)


def _project_qkv(h, w_in, a_q_norm, a_k_norm, b_q_norm, b_k_norm, b_f):
    m = h.shape[0]
    w = w_in.astype(BF16)
    o = 0
    cols = {}
    for name, width in (("aq", A_QK), ("ak", A_QK), ("av", A_V), ("bq", B_QK), ("bk", B_QK), ("bv", B_QK)):
        cols[name] = w[:, o:o + width]
        o += width
    w_f = jnp.pad(w[:, o:o + H_B], ((0, 0), (0, LANES - H_B)))
    b_f_pad = jnp.pad(b_f.astype(F32), (0, LANES - H_B)).reshape(1, LANES)

    def run(body, name, extra, dtypes):
        n = cols[name].shape[1]
        shapes = tuple(jax.ShapeDtypeStruct((m, n), dt) for dt in dtypes)
        specs = lambda tm: tuple(pl.BlockSpec((tm, n), lambda i: (i, 0)) for _ in dtypes)
        return _proj(body, h, cols[name], extra, shapes, specs)

    def normed(name, gain, group, scale, dtypes):
        g = jnp.tile(gain.astype(F32), LANES // group).reshape(1, LANES)
        return run(functools.partial(_proj_norm_body, group=group, scale=scale), name, [g], dtypes)

    (aq,) = normed("aq", a_q_norm, DA, DA ** -0.5, (BF16,))
    ak = normed("ak", a_k_norm, DA, 1.0, (F32, BF16))
    av = run(_proj_plain_body, "av", [], (F32, BF16))
    (bq,) = normed("bq", b_q_norm, DB, DB ** -0.5, (BF16,))
    bk = normed("bk", b_k_norm, DB, 1.0, (F32, BF16))
    bv = run(_proj_plain_body, "bv", [], (F32, BF16))
    logf, logf_t = _proj(
        _proj_logf_body, h, w_f, [b_f_pad],
        (jax.ShapeDtypeStruct((m, H_B), F32), jax.ShapeDtypeStruct((H_B, m), F32)),
        lambda tm: (pl.BlockSpec((tm, H_B), lambda i: (i, 0)), pl.BlockSpec((H_B, tm), lambda i: (0, i))))
    return aq, ak, av, bq, bk, bv, logf, logf_t


def _cumsum_body(x_ref, o_ref):
    x = x_ref[...]
    n = x.shape[1]
    lane = lax.broadcasted_iota(I32, x.shape, 1)
    shift = 1
    while shift < n:
        x = x + jnp.where(lane >= shift, pltpu.roll(x, shift, 1), 0.0)
        shift *= 2
    o_ref[...] = x


def _cumsum_lanes(x):
    b, h, s = x.shape
    return pl.pallas_call(
        _cumsum_body,
        grid=(b,),
        in_specs=[pl.BlockSpec((None, h, s), lambda i: (i, 0, 0))],
        out_specs=pl.BlockSpec((None, h, s), lambda i: (i, 0, 0)),
        out_shape=jax.ShapeDtypeStruct((b, h, s), F32),
        compiler_params=_params(1),
        name="cumsum",
    )(x)


def _nt_dot(a, b):
    return lax.dot_general(a, b, (((1,), (1,)), ((), ())), preferred_element_type=F32)


def _with_ones(v):
    return jnp.concatenate([v, jnp.ones_like(v)], axis=1)


def _flash_sweep(q, k_ref, v_ref, s_scr, m_scr, acc_scr, qi, bq, past_bias, diag_bias):
    n_rows = q.shape[0]
    buf_a, buf_b = s_scr.at[0], s_scr.at[1]

    def scores(tile, buf):
        k0 = pl.multiple_of(tile * bq, bq)
        k = k_ref[pl.ds(k0, bq), :]
        for r0 in range(0, n_rows, bq):
            buf[r0:r0 + bq, :] = _nt_dot(q[r0:r0 + bq, :], k)

    def consume(tile, buf, bias):
        k0 = pl.multiple_of(tile * bq, bq)
        v_aug = _with_ones(v_ref[pl.ds(k0, bq), :])
        for r0 in range(0, n_rows, bq):
            rows = slice(r0, r0 + bq)
            s = buf[rows, :] + bias
            m_prev = m_scr[rows, :]
            m_new = jnp.maximum(m_prev, jnp.max(s, axis=1, keepdims=True))
            alpha = jnp.exp(m_prev - m_new)
            p = jnp.exp(s - jnp.tile(m_new, (1, bq // LANES)))
            acc_scr[rows, :] = (jnp.tile(alpha, (1, 2)) * acc_scr[rows, :]
                                + jnp.dot(p.astype(BF16), v_aug, preferred_element_type=F32))
            m_scr[rows, :] = m_new

    scores(0, buf_a)

    def tile_pair(t, carry):
        j = 2 * t
        scores(j + 1, buf_b)
        consume(j, buf_a, past_bias(j))
        scores(j + 2, buf_a)
        consume(j + 1, buf_b, past_bias(j + 1))
        return carry

    lax.fori_loop(0, qi // 2, tile_pair, 0)

    @pl.when(qi % 2 == 0)
    def _():
        consume(qi, buf_a, diag_bias)

    @pl.when(qi % 2 == 1)
    def _():
        scores(qi, buf_b)
        consume(qi - 1, buf_a, past_bias(qi - 1))
        consume(qi, buf_b, diag_bias)


def _init_softmax(m_scr, acc_scr):
    m_scr[...] = jnp.full(m_scr.shape, NEG, F32)
    acc_scr[...] = jnp.zeros(acc_scr.shape, F32)


def _stack_components(q):
    lane = lax.broadcasted_iota(I32, q.shape, 1)
    zero = jnp.zeros_like(q)
    return jnp.concatenate([jnp.where(lane < DA, q, zero), jnp.where(lane >= DA, q, zero)], axis=0)


def _diff_combine(o1, o2, lam, gain):
    o = o1 - lam * o2
    inv = lax.rsqrt(jnp.mean(o * o, axis=1, keepdims=True) + NORM_EPS)
    return o * inv * gain * (1.0 - LAM_INIT)


def _first_step():
    return (pl.program_id(0) == 0) & (pl.program_id(1) == 0) & (pl.program_id(2) == 0)


def _attn_a_prompt_body(slope_ref, lam_ref, q_ref, k_ref, v_ref, g_ref, wi_ref, o_ref, wo_ref,
                        q_scr, rel_scr, s_scr, m_scr, acc_scr, *, bq):
    h = pl.program_id(1)
    qi = pl.program_id(2)
    slope = slope_ref[h]

    @pl.when(_first_step())
    def _():
        r = lax.broadcasted_iota(I32, (bq, bq), 0)
        c = lax.broadcasted_iota(I32, (bq, bq), 1)
        rel_scr[...] = jnp.where(c // CHUNK <= r // CHUNK, (r - jnp.abs(r - c)).astype(F32), NEG)

    wo_ref[...] = wi_ref[...].astype(wo_ref.dtype)
    q_scr[...] = _stack_components(q_ref[...])
    _init_softmax(m_scr, acc_scr)
    col = lax.broadcasted_iota(I32, (1, bq), 1).astype(F32)
    _flash_sweep(q_scr, k_ref, v_ref, s_scr, m_scr, acc_scr, qi, bq,
                 past_bias=lambda j: slope * (col + ((j - qi) * bq).astype(F32)),
                 diag_bias=slope * rel_scr[...])
    acc = acc_scr[...]
    o1 = acc[:bq, :LANES] / acc[:bq, LANES:]
    o2 = acc[bq:, :LANES] / acc[bq:, LANES:]
    o_ref[...] = _diff_combine(o1, o2, lam_ref[0], g_ref[...]).astype(o_ref.dtype)


def _cast_specs(w, steps, index):
    rows = w.shape[0] // steps
    assert rows * steps == w.shape[0] and rows % 16 == 0
    spec = pl.BlockSpec((rows, w.shape[1]), lambda b, h, i: (index(b, h, i), 0))
    return spec, spec, jax.ShapeDtypeStruct(w.shape, BF16)


def _attn_a_prompt(aq, ak, av, slopes, lam, gain, w_cast, batch, seq):
    bq = min(BQ_A, seq)
    nq = seq // bq
    smem = pl.BlockSpec(memory_space=pltpu.SMEM)
    w_in_spec, w_out_spec, w_shape = _cast_specs(w_cast, batch * H_A * nq, lambda b, h, i: (b * H_A + h) * nq + i)
    return pl.pallas_call(
        functools.partial(_attn_a_prompt_body, bq=bq),
        grid=(batch, H_A, nq),
        in_specs=[smem, smem,
                  pl.BlockSpec((bq, LANES), lambda b, h, i: (b * nq + i, h)),
                  pl.BlockSpec((seq, LANES), lambda b, h, i: (b, h)),
                  pl.BlockSpec((seq, LANES), lambda b, h, i: (b, h)),
                  pl.BlockSpec((1, LANES), lambda b, h, i: (0, 0)),
                  w_in_spec],
        out_specs=(pl.BlockSpec((bq, LANES), lambda b, h, i: (b * nq + i, h)), w_out_spec),
        out_shape=(jax.ShapeDtypeStruct(aq.shape, BF16), w_shape),
        scratch_shapes=[pltpu.VMEM((2 * bq, LANES), BF16), pltpu.VMEM((bq, bq), F32),
                        pltpu.VMEM((2, 2 * bq, bq), F32), pltpu.VMEM((2 * bq, LANES), F32),
                        pltpu.VMEM((2 * bq, 2 * LANES), F32)],
        compiler_params=_params(3),
        name="attn_a_prompt",
    )(slopes, lam, aq, ak, av, gain, w_cast)


def _attn_b_prompt_body(q_ref, k_ref, v_ref, c_ref, wi_ref, o_ref, wo_ref, mask_scr, s_scr, m_scr, acc_scr, *, bq):
    qi = pl.program_id(2)

    @pl.when(_first_step())
    def _():
        r = lax.broadcasted_iota(I32, (bq, bq), 0)
        c = lax.broadcasted_iota(I32, (bq, bq), 1)
        mask_scr[...] = jnp.where(c <= r, 0.0, NEG)

    wo_ref[...] = wi_ref[...].astype(wo_ref.dtype)
    _init_softmax(m_scr, acc_scr)
    c_tile = c_ref[pl.ds(qi, 1), :]
    c_first = c_tile[:, 0:1]
    _flash_sweep(q_ref, k_ref, v_ref, s_scr, m_scr, acc_scr, qi, bq,
                 past_bias=lambda j: c_first - c_ref[pl.ds(j, 1), :],
                 diag_bias=mask_scr[...] + (c_first - c_tile))
    acc = acc_scr[...]
    o_ref[...] = (acc[:, :LANES] / acc[:, LANES:]).astype(o_ref.dtype)


def _attn_b_prompt(bq_arr, bk, bv, c, w_cast, batch, seq):
    bq = min(BQ_B, seq)
    nq = seq // bq
    c4 = c.reshape(batch, H_B, nq, bq)
    w_in_spec, w_out_spec, w_shape = _cast_specs(w_cast, batch * H_B * nq, lambda b, h, i: (b * H_B + h) * nq + i)
    return pl.pallas_call(
        functools.partial(_attn_b_prompt_body, bq=bq),
        grid=(batch, H_B, nq),
        in_specs=[pl.BlockSpec((bq, LANES), lambda b, h, i: (b * nq + i, h)),
                  pl.BlockSpec((seq, LANES), lambda b, h, i: (b, h)),
                  pl.BlockSpec((seq, LANES), lambda b, h, i: (b, h)),
                  pl.BlockSpec((None, None, nq, bq), lambda b, h, i: (b, h, 0, 0)),
                  w_in_spec],
        out_specs=(pl.BlockSpec((bq, LANES), lambda b, h, i: (b * nq + i, h)), w_out_spec),
        out_shape=(jax.ShapeDtypeStruct(bq_arr.shape, BF16), w_shape),
        scratch_shapes=[pltpu.VMEM((bq, bq), F32), pltpu.VMEM((2, bq, bq), F32), pltpu.VMEM((bq, LANES), F32),
                        pltpu.VMEM((bq, 2 * LANES), F32)],
        compiler_params=_params(3),
        name="attn_b_prompt",
    )(bq_arr, bk, bv, c4, w_cast)


def _two_part_softmax(s_c, s_n, v_c, v_n):
    m = jnp.maximum(jnp.max(s_c, axis=1, keepdims=True), jnp.max(s_n, axis=1, keepdims=True))
    p_c = jnp.exp(s_c - m)
    p_n = jnp.exp(s_n - m)
    l = jnp.sum(p_c, axis=1, keepdims=True) + jnp.sum(p_n, axis=1, keepdims=True)
    acc = (jnp.dot(p_c.astype(BF16), v_c, preferred_element_type=F32)
           + jnp.dot(p_n.astype(BF16), v_n, preferred_element_type=F32))
    return acc / l


def _attn_a_sample_body(slope_ref, lam_ref, q_ref, kc_ref, vc_ref, kn_ref, vn_ref, g_ref, o_ref, *, past, sq):
    col_c = lax.broadcasted_iota(I32, (1, past), 1).astype(F32) - float(past)
    r = lax.broadcasted_iota(I32, (2 * sq, sq), 0)
    r = jnp.where(r >= sq, r - sq, r)
    c = lax.broadcasted_iota(I32, (2 * sq, sq), 1)
    rel_n = (r - jnp.abs(r - c)).astype(F32)
    for h in range(H_A):
        hs = slice(h * LANES, (h + 1) * LANES)
        slope = slope_ref[h]
        qq = _stack_components(q_ref[:, hs])
        s_c = _nt_dot(qq, kc_ref[pl.ds(h, past, stride=H_A), :].astype(BF16)) + slope * col_c
        s_n = _nt_dot(qq, kn_ref[:, hs]) + slope * rel_n
        o = _two_part_softmax(s_c, s_n, vc_ref[pl.ds(h, past, stride=H_A), :].astype(BF16), vn_ref[:, hs])
        o_ref[:, hs] = _diff_combine(o[:sq], o[sq:], lam_ref[0], g_ref[...]).astype(o_ref.dtype)


def _attn_b_sample_body(q_ref, kc_ref, vc_ref, kn_ref, vn_ref, c_ref, o_ref, *, past, sq):
    r = lax.broadcasted_iota(I32, (sq, sq), 0)
    c = lax.broadcasted_iota(I32, (sq, sq), 1)
    causal = c <= r
    for h in range(H_B):
        hs = slice(h * LANES, (h + 1) * LANES)
        q = q_ref[:, hs]
        c_row = c_ref[h:h + 1, :]
        c_first = c_row[:, past:past + 1]
        s_c = _nt_dot(q, kc_ref[pl.ds(h, past, stride=H_B), :].astype(BF16)) + (c_first - c_row[:, :past])
        s_n = _nt_dot(q, kn_ref[:, hs]) + (c_first - c_row[:, past:past + sq])
        s_n = jnp.where(causal, s_n, NEG)
        o = _two_part_softmax(s_c, s_n, vc_ref[pl.ds(h, past, stride=H_B), :].astype(BF16), vn_ref[:, hs])
        o_ref[:, hs] = o.astype(o_ref.dtype)


def _attn_sample(body, q, kc, vc, kn, vn, tail, smem_args, batch, past, sq):
    width = q.shape[1]
    smem = pl.BlockSpec(memory_space=pltpu.SMEM)
    cache = pl.BlockSpec((None,) + kc.shape[1:], lambda b: (b, 0, 0))
    in_specs = [smem] * len(smem_args) + [
        pl.BlockSpec((sq, width), lambda b: (b, 0)), cache, cache,
        pl.BlockSpec((sq, width), lambda b: (b, 0)),
        pl.BlockSpec((sq, width), lambda b: (b, 0))]
    if tail.ndim == 3:
        in_specs.append(pl.BlockSpec((None,) + tail.shape[1:], lambda b: (b, 0, 0)))
    else:
        in_specs.append(pl.BlockSpec(tail.shape, lambda b: (0, 0)))
    return pl.pallas_call(
        functools.partial(body, past=past, sq=sq),
        grid=(batch,),
        in_specs=in_specs,
        out_specs=pl.BlockSpec((sq, width), lambda b: (b, 0)),
        out_shape=jax.ShapeDtypeStruct(q.shape, BF16),
        compiler_params=_params(1),
        name="attn_sample",
    )(*smem_args, q, kc, vc, kn, vn, tail)


def _merge_body(x_ref, h_ref, oa_ref, ob_ref, woa_ref, wob_ref, wga_ref, wgb_ref, bga_ref, bgb_ref, wout_ref,
                o_ref, acc_scr):
    j = pl.program_id(1)

    @pl.when(j == 0)
    def _():
        acc_scr[...] = jnp.zeros(acc_scr.shape, F32)

    h = h_ref[...]
    y_a = jnp.dot(oa_ref[...], woa_ref[...], preferred_element_type=F32)
    y_b = jnp.dot(ob_ref[...], wob_ref[...], preferred_element_type=F32)
    g_a = jax.nn.sigmoid(jnp.dot(h, wga_ref[...], preferred_element_type=F32) + bga_ref[...])
    g_b = jax.nn.sigmoid(jnp.dot(h, wgb_ref[...], preferred_element_type=F32) + bgb_ref[...])
    merged = (g_a * y_a + g_b * y_b).astype(BF16)
    acc_scr[...] += jnp.dot(merged, wout_ref[...], preferred_element_type=F32)

    @pl.when(j == pl.num_programs(1) - 1)
    def _():
        o_ref[...] = x_ref[...] + acc_scr[...]


def _merge(x, h, o_a, o_b, w_o_a, w_o_b, w_ga, w_gb, b_ga, b_gb, w_out):
    m, d = x.shape
    tm = min(TM_MERGE, m)
    tn = TN_MERGE
    row = lambda n: pl.BlockSpec((tm, n), lambda i, j: (i, 0))
    colw = lambda k: pl.BlockSpec((k, tn), lambda i, j: (0, j))
    return pl.pallas_call(
        _merge_body,
        grid=(m // tm, d // tn),
        in_specs=[row(d), row(d), row(A_V), row(B_QK), colw(A_V), colw(B_QK), colw(d), colw(d),
                  pl.BlockSpec((1, tn), lambda i, j: (0, j)), pl.BlockSpec((1, tn), lambda i, j: (0, j)),
                  pl.BlockSpec((tn, d), lambda i, j: (j, 0))],
        out_specs=row(d),
        out_shape=jax.ShapeDtypeStruct((m, d), F32),
        scratch_shapes=[pltpu.VMEM((tm, d), F32)],
        compiler_params=_params(2),
        name="merge",
    )(x, h, o_a, o_b, w_o_a, w_o_b, w_ga, w_gb, b_ga, b_gb, w_out)


def _router_body(x_ref, g_ref, whi_ref, wlo_ref, b_ref, cin_ref, sel_ref, gate_ref, cout_ref, cnt_scr):
    i = pl.program_id(0)

    @pl.when(i == 0)
    def _():
        cnt_scr[...] = cin_ref[...]

    x = x_ref[...]
    tm = x.shape[0]
    h = x * lax.rsqrt(jnp.mean(x * x, axis=-1, keepdims=True) + NORM_EPS) * g_ref[...]
    hi = h.astype(BF16)
    lo = (h - hi.astype(F32)).astype(BF16)
    logits = (jnp.dot(hi, whi_ref[...], preferred_element_type=F32)
              + jnp.dot(hi, wlo_ref[...], preferred_element_type=F32)
              + jnp.dot(lo, whi_ref[...], preferred_element_type=F32)) + b_ref[...]

    lane = lax.broadcasted_iota(I32, logits.shape, 1)
    work = logits
    picks, vals, ids = [], [], []
    for _ in range(TOP_K):
        mx = jnp.max(work, axis=1, keepdims=True)
        idx = jnp.min(jnp.where(work == mx, lane, LANES), axis=1, keepdims=True)
        pick = lane == idx
        work = jnp.where(pick, -jnp.inf, work)
        picks.append(pick)
        vals.append(mx)
        ids.append(idx)

    exps = [jnp.exp(v - vals[0]) for v in vals]
    denom = exps[0] + exps[1] + exps[2] + exps[3]

    picked = jnp.zeros(logits.shape, F32)
    for pick in picks:
        picked = picked + jnp.where(pick, 1.0, 0.0)
    r = lax.broadcasted_iota(I32, (tm, tm), 0)
    c = lax.broadcasted_iota(I32, (tm, tm), 1)
    earlier = jnp.where(c < r, 1.0, 0.0).astype(BF16)
    before = jnp.dot(earlier, picked.astype(BF16), preferred_element_type=F32) + cnt_scr[...]

    sel = jnp.zeros(logits.shape, I32)
    gate = jnp.zeros(logits.shape, F32)
    for k in range(TOP_K):
        rank = jnp.sum(jnp.where(picks[k], before, 0.0), axis=1, keepdims=True).astype(I32)
        sel = jnp.where(lane == k, ids[k], sel)
        sel = jnp.where(lane == TOP_K + k, rank, sel)
        gate = jnp.where(lane == k, exps[k] / denom, gate)
    sel_ref[...] = sel
    gate_ref[...] = gate
    cnt_scr[...] += jnp.sum(picked, axis=0, keepdims=True)
    cout_ref[...] = cnt_scr[...]


def _router(x, g, w_hi, w_lo, b, counts_in):
    m, d = x.shape
    tm = min(TM_ROUTE, m)
    const = lambda shape: pl.BlockSpec(shape, lambda i: (0, 0))
    return pl.pallas_call(
        _router_body,
        grid=(m // tm,),
        in_specs=[pl.BlockSpec((tm, d), lambda i: (i, 0)), const((1, d)), const((d, LANES)), const((d, LANES)),
                  const((1, LANES)), const((1, LANES))],
        out_specs=(pl.BlockSpec((tm, LANES), lambda i: (i, 0)), pl.BlockSpec((tm, LANES), lambda i: (i, 0)),
                   const((1, LANES))),
        out_shape=(jax.ShapeDtypeStruct((m, LANES), I32), jax.ShapeDtypeStruct((m, LANES), F32),
                   jax.ShapeDtypeStruct((1, LANES), F32)),
        scratch_shapes=[pltpu.VMEM((1, LANES), F32)],
        compiler_params=_params(1),
        name="router",
    )(x, g, w_hi, w_lo, b, counts_in)


def _dispatch_body(fill_ref, dest_ref, xp_ref, xs_ref, o_ref, zero_scr, sem, fill_sem, *, tt, tiles_p, tm,
                   n_blocks):
    i = pl.program_id(0)

    @pl.when(i == 0)
    def _():
        zero_scr[...] = jnp.zeros(zero_scr.shape, F32)
        for e in range(N_EXPERTS):
            @pl.when(fill_ref[e] >= 0)
            def _():
                off = pl.multiple_of(fill_ref[e], tm)
                pltpu.make_async_copy(zero_scr, o_ref.at[pl.ds(off, tm)], fill_sem).start()
        for e in range(N_EXPERTS):
            @pl.when(fill_ref[e] >= 0)
            def _():
                pltpu.make_async_copy(zero_scr, o_ref.at[pl.ds(0, tm)], fill_sem).wait()

        def zero_block(b, carry):
            off = pl.multiple_of(b * tm, tm)
            pltpu.make_async_copy(zero_scr, o_ref.at[pl.ds(off, tm)], fill_sem).start()
            return carry

        def wait_block(b, carry):
            pltpu.make_async_copy(zero_scr, o_ref.at[pl.ds(0, tm)], fill_sem).wait()
            return carry

        lax.fori_loop(fill_ref[N_EXPERTS], n_blocks, zero_block, 0)
        lax.fori_loop(fill_ref[N_EXPERTS], n_blocks, wait_block, 0)

    def move(src_ref):
        def issue(t, carry):
            for k in range(TOP_K):
                d = dest_ref[t * TOP_K + k]
                pltpu.make_async_copy(src_ref.at[pl.ds(t, 1)], o_ref.at[pl.ds(d, 1)], sem).start()
            return carry

        lax.fori_loop(0, tt, issue, 0)

        for k in range(TOP_K):
            pltpu.make_async_copy(src_ref, o_ref.at[pl.ds(0, tt)], sem).wait()

    @pl.when(i < tiles_p)
    def _():
        move(xp_ref)

    @pl.when(i >= tiles_p)
    def _():
        move(xs_ref)


def _dispatch(fill, dest_flat, x_p, x_s, rows, tm):
    m_p, d = x_p.shape
    m_s = x_s.shape[0]
    tt = TT_MOVE
    tiles_p = m_p // tt
    tiles = tiles_p + m_s // tt
    return pl.pallas_call(
        functools.partial(_dispatch_body, tt=tt, tiles_p=tiles_p, tm=tm, n_blocks=rows // tm),
        grid=(tiles,),
        in_specs=[pl.BlockSpec(memory_space=pltpu.SMEM),
                  pl.BlockSpec((tt * TOP_K,), lambda i: (i,), memory_space=pltpu.SMEM),
                  pl.BlockSpec((tt, d), lambda i: (jnp.minimum(i, tiles_p - 1), 0)),
                  pl.BlockSpec((tt, d), lambda i: (jnp.maximum(i - tiles_p, 0), 0))],
        out_specs=pl.BlockSpec(memory_space=pl.ANY),
        out_shape=jax.ShapeDtypeStruct((rows, d), F32),
        scratch_shapes=[pltpu.VMEM((tm, d), F32), pltpu.SemaphoreType.DMA, pltpu.SemaphoreType.DMA],
        compiler_params=pltpu.CompilerParams(dimension_semantics=("arbitrary",), vmem_limit_bytes=VMEM_LIMIT,
                                             has_side_effects=True),
        name="dispatch",
    )(fill, dest_flat, x_p, x_s)


def _experts_body(blk_ref, used_ref, x_ref, g_ref, wg_ref, wu_ref, bg_ref, bu_ref, wd_ref, bd_ref, o_ref,
                  h_scr, acc_scr):
    b = pl.program_id(0)
    f = pl.program_id(1)

    @pl.when(b < used_ref[0])
    def _():
        @pl.when(f == 0)
        def _():
            x = x_ref[...]
            inv = lax.rsqrt(jnp.mean(x * x, axis=-1, keepdims=True) + NORM_EPS)
            h_scr[...] = (x * inv * g_ref[...]).astype(BF16)
            acc_scr[...] = jnp.zeros(acc_scr.shape, F32)

        h = h_scr[...]
        g = jnp.dot(h, wg_ref[...], preferred_element_type=F32) + bg_ref[...]
        u = jnp.dot(h, wu_ref[...], preferred_element_type=F32) + bu_ref[...]
        g = jnp.minimum(g, SWIGLU_LIMIT)
        u = jnp.clip(u, -SWIGLU_LIMIT, SWIGLU_LIMIT)
        act = (u + 1.0) * g * jax.nn.sigmoid(SWIGLU_ALPHA * g)
        acc_scr[...] += jnp.dot(act.astype(BF16), wd_ref[...], preferred_element_type=F32)

        @pl.when(f == pl.num_programs(1) - 1)
        def _():
            o_ref[...] = acc_scr[...] + bd_ref[...]

    @pl.when((b >= used_ref[0]) & (f == 0))
    def _():
        o_ref[...] = jnp.zeros(o_ref.shape, F32)


def _experts(blk_e, n_used, xs, g_ffn, w_gu, b_gu, w_d, b_d, tm):
    rows, d = xs.shape
    tf = TF_EXP
    nf = D_FF // tf
    n_blocks = rows // tm

    def blk(b, used):
        return jnp.minimum(b, used[0] - 1)

    def fch(b, f, used):
        return jnp.where(b < used[0], f, nf - 1)

    grid_spec = pltpu.PrefetchScalarGridSpec(
        num_scalar_prefetch=2,
        grid=(n_blocks, nf),
        in_specs=[
            pl.BlockSpec((tm, d), lambda b, f, e, u: (blk(b, u), 0)),
            pl.BlockSpec((1, d), lambda b, f, e, u: (0, 0)),
            pl.BlockSpec((None, d, tf), lambda b, f, e, u: (e[blk(b, u)], 0, fch(b, f, u))),
            pl.BlockSpec((None, d, tf), lambda b, f, e, u: (e[blk(b, u)], 0, nf + fch(b, f, u))),
            pl.BlockSpec((None, 1, tf), lambda b, f, e, u: (e[blk(b, u)], 0, fch(b, f, u))),
            pl.BlockSpec((None, 1, tf), lambda b, f, e, u: (e[blk(b, u)], 0, nf + fch(b, f, u))),
            pl.BlockSpec((None, tf, d), lambda b, f, e, u: (e[blk(b, u)], fch(b, f, u), 0)),
            pl.BlockSpec((None, 1, d), lambda b, f, e, u: (e[blk(b, u)], 0, 0)),
        ],
        out_specs=pl.BlockSpec((tm, d), lambda b, f, e, u: (b, 0)),
        scratch_shapes=[pltpu.VMEM((tm, d), BF16), pltpu.VMEM((tm, d), F32)],
    )
    return pl.pallas_call(
        _experts_body,
        grid_spec=grid_spec,
        out_shape=jax.ShapeDtypeStruct((rows, d), F32),
        compiler_params=_params(2),
        name="experts",
    )(blk_e, n_used, xs, g_ffn, w_gu, w_gu, b_gu, b_gu, w_d, b_d)


def _combine_body(dest_ref, x_ref, gate_ref, ys_ref, o_ref, buf, sem, *, tt):
    def issue(t, carry):
        for k in range(TOP_K):
            d = dest_ref[t * TOP_K + k]
            pltpu.make_async_copy(ys_ref.at[pl.ds(d, 1)], buf.at[k, pl.ds(t, 1)], sem).start()
        return carry

    lax.fori_loop(0, tt, issue, 0)

    for k in range(TOP_K):
        pltpu.make_async_copy(ys_ref.at[pl.ds(0, tt)], buf.at[k], sem).wait()

    gate = gate_ref[...]
    out = x_ref[...]
    for k in range(TOP_K):
        out = out + gate[:, k:k + 1] * buf[k]
    o_ref[...] = out


def _combine(dest_flat, x, gate, ys):
    m, d = x.shape
    tt = TT_MOVE
    return pl.pallas_call(
        functools.partial(_combine_body, tt=tt),
        grid=(m // tt,),
        in_specs=[pl.BlockSpec((tt * TOP_K,), lambda i: (i,), memory_space=pltpu.SMEM),
                  pl.BlockSpec((tt, d), lambda i: (i, 0)),
                  pl.BlockSpec((tt, LANES), lambda i: (i, 0)),
                  pl.BlockSpec(memory_space=pl.ANY)],
        out_specs=pl.BlockSpec((tt, d), lambda i: (i, 0)),
        out_shape=jax.ShapeDtypeStruct((m, d), F32),
        scratch_shapes=[pltpu.VMEM((TOP_K, tt, d), F32), pltpu.SemaphoreType.DMA],
        compiler_params=_params(1),
        name="combine",
    )(dest_flat, x, gate, ys)


def _moe(x_p, x_s, g_ffn, w_router, b_router, w_gate_up, b_gate_up, w_down, b_down):
    d = x_p.shape[1]
    tm = TM_EXP
    g = g_ffn.astype(F32).reshape(1, d)
    w_r = jnp.pad(w_router.astype(F32), ((0, 0), (0, LANES - N_EXPERTS)))
    w_hi = w_r.astype(BF16)
    w_lo = (w_r - w_hi.astype(F32)).astype(BF16)
    b_r = jnp.pad(b_router.astype(F32), (0, LANES - N_EXPERTS), constant_values=NEG).reshape(1, LANES)

    sel_p, gate_p, counts = _router(x_p, g, w_hi, w_lo, b_r, jnp.zeros((1, LANES), F32))
    sel_s, gate_s, counts = _router(x_s, g, w_hi, w_lo, b_r, counts)

    counts = counts[0, :N_EXPERTS].astype(I32)
    padded = (counts + tm - 1) // tm * tm
    pad_end = jnp.cumsum(padded)
    pad_start = pad_end - padded
    n_assign = (x_p.shape[0] + x_s.shape[0]) * TOP_K
    n_blocks = -(-n_assign // tm) + N_EXPERTS
    rows = n_blocks * tm
    blk_start = jnp.arange(n_blocks, dtype=I32) * tm
    blk_e = jnp.minimum(jnp.sum(pad_end[None, :] <= blk_start[:, None], axis=1), N_EXPERTS - 1).astype(I32)
    n_used = (pad_end[-1:] // tm).astype(I32)
    fill = jnp.concatenate([jnp.where(counts > 0, pad_end - tm, -1), n_used]).astype(I32)

    def dest_of(sel):
        return (pad_start[sel[:, :TOP_K]] + sel[:, TOP_K:2 * TOP_K]).reshape(-1)

    dest_p, dest_s = dest_of(sel_p), dest_of(sel_s)
    xs = _dispatch(fill, jnp.concatenate([dest_p, dest_s]), x_p, x_s, rows, tm)
    ys = _experts(blk_e, n_used, xs, g, w_gate_up, b_gate_up.astype(F32)[:, None, :],
                  w_down, b_down.astype(F32)[:, None, :], tm)
    return _combine(dest_p, x_p, gate_p, ys), _combine(dest_s, x_s, gate_s, ys)


def kernel(x_prompt, x_sample, cache_a_k, cache_a_v, cache_b_k, cache_b_v, cache_b_logf, g_attn, w_in, b_gate,
           a_q_norm, a_k_norm, b_q_norm, b_k_norm, b_f, lambda_q1, lambda_k1, lambda_q2, lambda_k2, a_subln,
           w_o_a, w_o_b, w_out, g_ffn, w_router, b_router, w_gate_up, b_gate_up, w_down, b_down):
    depth = g_attn.shape[0]
    assert depth == 1, "single-layer trunk"
    batch, seq, d = x_prompt.shape
    dec_batch, dec_seq, _ = x_sample.shape
    past = cache_a_k.shape[2]
    assert dec_seq == CHUNK and past % CHUNK == 0, "the new sample frames must form exactly one chunk"
    sk = past + dec_seq
    sk_pad = -(-sk // LANES) * LANES

    xp = x_prompt.reshape(batch * seq, d)
    xs = x_sample.reshape(dec_batch * dec_seq, d)
    g_a = g_attn[0].astype(F32)
    w_in0 = w_in[0]
    split = 2 * A_QK + A_V + 3 * B_QK + H_B
    w_ga = w_in0[:, split:split + d].astype(BF16)
    w_gb = w_in0[:, split + d:split + 2 * d].astype(BF16)
    b_ga = b_gate[0, :d].astype(F32).reshape(1, d)
    b_gb = b_gate[0, d:].astype(F32).reshape(1, d)
    w_oa, w_ob, w_o = w_o_a[0].astype(BF16), w_o_b[0].astype(BF16), w_out[0].astype(BF16)

    slopes = jnp.exp2(-8.0 * jnp.arange(1, H_A + 1, dtype=F32) / H_A)
    lam = (jnp.exp(jnp.sum(lambda_q1[0].astype(F32) * lambda_k1[0].astype(F32)))
           - jnp.exp(jnp.sum(lambda_q2[0].astype(F32) * lambda_k2[0].astype(F32))) + LAM_INIT).reshape(1)
    subln = a_subln[0].astype(F32).reshape(1, DV_A)

    norms = (a_q_norm[0], a_k_norm[0], b_q_norm[0], b_k_norm[0], b_f[0])

    hp = _rms_bf16(xp, g_a)
    aq, (pak, ak16), (pav, av16), bq, (pbk, bk16), (pbv, bv16), plogf, plogf_t = _project_qkv(hp, w_in0, *norms)
    cp = _cumsum_lanes(plogf_t.reshape(H_B, batch, seq).transpose(1, 0, 2))
    n_e, _, two_f = w_gate_up[0].shape
    oa_p, w_gu16 = _attn_a_prompt(aq, ak16, av16, slopes, lam, subln, w_gate_up[0].reshape(n_e * d, two_f),
                                  batch, seq)
    ob_p, w_d16 = _attn_b_prompt(bq, bk16, bv16, cp, w_down[0].reshape(n_e * D_FF, d), batch, seq)
    x2_p = _merge(xp, hp, oa_p, ob_p, w_oa, w_ob, w_ga, w_gb, b_ga, b_gb, w_o)

    hs = _rms_bf16(xs, g_a)
    aq, (sak, ak16), (sav, av16), bq, (sbk, bk16), (sbv, bv16), slogf, slogf_t = _project_qkv(hs, w_in0, *norms)
    logf_all = jnp.concatenate(
        [cache_b_logf[0].astype(F32).transpose(0, 2, 1),
         slogf_t.reshape(H_B, dec_batch, dec_seq).transpose(1, 0, 2),
         jnp.zeros((dec_batch, H_B, sk_pad - sk), F32)], axis=2)
    cs = _cumsum_lanes(logf_all)
    rows_of = lambda cache: cache[0].reshape(dec_batch, past * cache.shape[3], LANES)
    oa_s = _attn_sample(_attn_a_sample_body, aq, rows_of(cache_a_k), rows_of(cache_a_v), ak16, av16, subln,
                        (slopes, lam), dec_batch, past, dec_seq)
    ob_s = _attn_sample(_attn_b_sample_body, bq, rows_of(cache_b_k), rows_of(cache_b_v), bk16, bv16, cs, (),
                        dec_batch, past, dec_seq)
    x2_s = _merge(xs, hs, oa_s, ob_s, w_oa, w_ob, w_ga, w_gb, b_ga, b_gb, w_o)

    y_p, y_s = _moe(x2_p, x2_s, g_ffn[0], w_router[0], b_router[0], w_gu16.reshape(n_e, d, two_f), b_gate_up[0],
                    w_d16.reshape(n_e, D_FF, d), b_down[0])

    def heads(a, b, s, h):
        return a.reshape(1, b, s, h, -1)

    return (y_p.reshape(batch, seq, d), y_s.reshape(dec_batch, dec_seq, d),
            heads(pak, batch, seq, H_A), heads(pav, batch, seq, H_A), heads(pbk, batch, seq, H_B),
            heads(pbv, batch, seq, H_B), plogf.reshape(1, batch, seq, H_B),
            heads(sak, dec_batch, dec_seq, H_A), heads(sav, dec_batch, dec_seq, H_A),
            heads(sbk, dec_batch, dec_seq, H_B), heads(sbv, dec_batch, dec_seq, H_B),
            slogf.reshape(1, dec_batch, dec_seq, H_B))
```

```python
import functools
import math

import jax
import jax.numpy as jnp
from jax import lax
from jax.experimental import pallas as pl
from jax.experimental.pallas import tpu as pltpu

F32 = jnp.float32
BF16 = jnp.bfloat16
I32 = jnp.int32

D_MODEL = 2048
CHUNK = 64
H_A, DA, DV_A = 8, 64, 128
H_B, DB = 8, 128
A_QK, A_V, B_QK = H_A * 2 * DA, H_A * DV_A, H_B * DB
N_EXPERTS, TOP_K, D_FF = 32, 4, 2048
SWIGLU_LIMIT, SWIGLU_ALPHA = 7.0, 1.702
NORM_EPS = 1e-6
LAM_INIT = 0.8 - 0.6 * math.exp(-0.3 * 0)

LANES = 128
VMEM_LIMIT = 56 * 1024 * 1024
NEG = -1e30

TM_NORM = 512
TM_PROJ = 512
BQ_A = 512
BQ_B = 512
TM_MERGE = 512
TN_MERGE = 512
TM_ROUTE = 512
TT_MOVE = 256
TM_EXP = 512
TF_EXP = 1024


def _params(n_axes):
    return pltpu.CompilerParams(dimension_semantics=("arbitrary",) * n_axes, vmem_limit_bytes=VMEM_LIMIT)


def _rms_body(x_ref, g_ref, o_ref):
    x = x_ref[...]
    ms = jnp.mean(x * x, axis=-1, keepdims=True)
    o_ref[...] = (x * lax.rsqrt(ms + NORM_EPS) * g_ref[...]).astype(o_ref.dtype)


def _rms_bf16(x, g):
    m, d = x.shape
    tm = min(TM_NORM, m)
    return pl.pallas_call(
        _rms_body,
        grid=(m // tm,),
        in_specs=[pl.BlockSpec((tm, d), lambda i: (i, 0)), pl.BlockSpec((1, d), lambda i: (0, 0))],
        out_specs=pl.BlockSpec((tm, d), lambda i: (i, 0)),
        out_shape=jax.ShapeDtypeStruct((m, d), BF16),
        compiler_params=_params(1),
        name="rms_norm",
    )(x, g.reshape(1, d))


def _store_all(o_refs, cols, val):
    for o_ref in o_refs:
        o_ref[:, cols] = val.astype(o_ref.dtype)


def _proj_plain_body(h_ref, w_ref, *o_refs):
    _store_all(o_refs, slice(None), jnp.dot(h_ref[...], w_ref[...], preferred_element_type=F32))


def _proj_norm_body(h_ref, w_ref, g_ref, *o_refs, group, scale):
    y = jnp.dot(h_ref[...], w_ref[...], preferred_element_type=F32)
    lane = lax.broadcasted_iota(I32, (y.shape[0], LANES), 1)
    gain = g_ref[...] * scale
    for s in range(y.shape[1] // LANES):
        cols = slice(s * LANES, (s + 1) * LANES)
        slab = y[:, cols]
        sq = slab * slab
        if group == LANES:
            inv = lax.rsqrt(jnp.mean(sq, axis=1, keepdims=True) + NORM_EPS)
        else:
            lo = lane < group
            s_lo = jnp.sum(jnp.where(lo, sq, 0.0), axis=1, keepdims=True)
            s_hi = jnp.sum(jnp.where(lo, 0.0, sq), axis=1, keepdims=True)
            inv = jnp.where(lo, lax.rsqrt(s_lo / group + NORM_EPS), lax.rsqrt(s_hi / group + NORM_EPS))
        _store_all(o_refs, cols, slab * inv * gain)


def _proj_logf_body(h_ref, w_ref, b_ref, o_ref, ot_ref):
    z = jnp.dot(h_ref[...], w_ref[...], preferred_element_type=F32) + b_ref[...]
    logf = jnp.minimum(z, 0.0) - jnp.log1p(jnp.exp(-jnp.abs(z)))
    o_ref[...] = logf[:, :H_B]
    ot_ref[...] = logf.T[:H_B, :]


def _proj(body, h, w, extra, out_shapes, out_specs):
    m, d = h.shape
    n = w.shape[1]
    tm = min(TM_PROJ, m)
    in_specs = [pl.BlockSpec((tm, d), lambda i: (i, 0)), pl.BlockSpec((d, n), lambda i: (0, 0))]
    in_specs += [pl.BlockSpec(e.shape, lambda i: (0, 0)) for e in extra]
    return pl.pallas_call(
        body,
        grid=(m // tm,),
        in_specs=in_specs,
        out_specs=out_specs(tm),
        out_shape=out_shapes,
        compiler_params=_params(1),
        name="proj",
    )(h, w, *extra)


def _project_qkv(h, w_in, a_q_norm, a_k_norm, b_q_norm, b_k_norm, b_f):
    m = h.shape[0]
    w = w_in.astype(BF16)
    o = 0
    cols = {}
    for name, width in (("aq", A_QK), ("ak", A_QK), ("av", A_V), ("bq", B_QK), ("bk", B_QK), ("bv", B_QK)):
        cols[name] = w[:, o:o + width]
        o += width
    w_f = jnp.pad(w[:, o:o + H_B], ((0, 0), (0, LANES - H_B)))
    b_f_pad = jnp.pad(b_f.astype(F32), (0, LANES - H_B)).reshape(1, LANES)

    def run(body, name, extra, dtypes):
        n = cols[name].shape[1]
        shapes = tuple(jax.ShapeDtypeStruct((m, n), dt) for dt in dtypes)
        specs = lambda tm: tuple(pl.BlockSpec((tm, n), lambda i: (i, 0)) for _ in dtypes)
        return _proj(body, h, cols[name], extra, shapes, specs)

    def normed(name, gain, group, scale, dtypes):
        g = jnp.tile(gain.astype(F32), LANES // group).reshape(1, LANES)
        return run(functools.partial(_proj_norm_body, group=group, scale=scale), name, [g], dtypes)

    (aq,) = normed("aq", a_q_norm, DA, DA ** -0.5, (BF16,))
    ak = normed("ak", a_k_norm, DA, 1.0, (F32, BF16))
    av = run(_proj_plain_body, "av", [], (F32, BF16))
    (bq,) = normed("bq", b_q_norm, DB, DB ** -0.5, (BF16,))
    bk = normed("bk", b_k_norm, DB, 1.0, (F32, BF16))
    bv = run(_proj_plain_body, "bv", [], (F32, BF16))
    logf, logf_t = _proj(
        _proj_logf_body, h, w_f, [b_f_pad],
        (jax.ShapeDtypeStruct((m, H_B), F32), jax.ShapeDtypeStruct((H_B, m), F32)),
        lambda tm: (pl.BlockSpec((tm, H_B), lambda i: (i, 0)), pl.BlockSpec((H_B, tm), lambda i: (0, i))))
    return aq, ak, av, bq, bk, bv, logf, logf_t


def _cumsum_body(x_ref, o_ref):
    x = x_ref[...]
    n = x.shape[1]
    lane = lax.broadcasted_iota(I32, x.shape, 1)
    shift = 1
    while shift < n:
        x = x + jnp.where(lane >= shift, pltpu.roll(x, shift, 1), 0.0)
        shift *= 2
    o_ref[...] = x


def _cumsum_lanes(x):
    b, h, s = x.shape
    return pl.pallas_call(
        _cumsum_body,
        grid=(b,),
        in_specs=[pl.BlockSpec((None, h, s), lambda i: (i, 0, 0))],
        out_specs=pl.BlockSpec((None, h, s), lambda i: (i, 0, 0)),
        out_shape=jax.ShapeDtypeStruct((b, h, s), F32),
        compiler_params=_params(1),
        name="cumsum",
    )(x)


def _nt_dot(a, b):
    return lax.dot_general(a, b, (((1,), (1,)), ((), ())), preferred_element_type=F32)


def _with_ones(v):
    return jnp.concatenate([v, jnp.ones_like(v)], axis=1)


def _flash_sweep(q, k_ref, v_ref, s_scr, m_scr, acc_scr, qi, bq, past_bias, diag_bias):
    n_rows = q.shape[0]
    buf_a, buf_b = s_scr.at[0], s_scr.at[1]

    def scores(tile, buf):
        k0 = pl.multiple_of(tile * bq, bq)
        k = k_ref[pl.ds(k0, bq), :]
        for r0 in range(0, n_rows, bq):
            buf[r0:r0 + bq, :] = _nt_dot(q[r0:r0 + bq, :], k)

    def consume(tile, buf, bias):
        k0 = pl.multiple_of(tile * bq, bq)
        v_aug = _with_ones(v_ref[pl.ds(k0, bq), :])
        for r0 in range(0, n_rows, bq):
            rows = slice(r0, r0 + bq)
            s = buf[rows, :] + bias
            m_prev = m_scr[rows, :]
            m_new = jnp.maximum(m_prev, jnp.max(s, axis=1, keepdims=True))
            alpha = jnp.exp(m_prev - m_new)
            p = jnp.exp(s - jnp.tile(m_new, (1, bq // LANES)))
            acc_scr[rows, :] = (jnp.tile(alpha, (1, 2)) * acc_scr[rows, :]
                                + jnp.dot(p.astype(BF16), v_aug, preferred_element_type=F32))
            m_scr[rows, :] = m_new

    scores(0, buf_a)

    def tile_pair(t, carry):
        j = 2 * t
        scores(j + 1, buf_b)
        consume(j, buf_a, past_bias(j))
        scores(j + 2, buf_a)
        consume(j + 1, buf_b, past_bias(j + 1))
        return carry

    lax.fori_loop(0, qi // 2, tile_pair, 0)

    @pl.when(qi % 2 == 0)
    def _():
        consume(qi, buf_a, diag_bias)

    @pl.when(qi % 2 == 1)
    def _():
        scores(qi, buf_b)
        consume(qi - 1, buf_a, past_bias(qi - 1))
        consume(qi, buf_b, diag_bias)


def _init_softmax(m_scr, acc_scr):
    m_scr[...] = jnp.full(m_scr.shape, NEG, F32)
    acc_scr[...] = jnp.zeros(acc_scr.shape, F32)


def _stack_components(q):
    lane = lax.broadcasted_iota(I32, q.shape, 1)
    zero = jnp.zeros_like(q)
    return jnp.concatenate([jnp.where(lane < DA, q, zero), jnp.where(lane >= DA, q, zero)], axis=0)


def _diff_combine(o1, o2, lam, gain):
    o = o1 - lam * o2
    inv = lax.rsqrt(jnp.mean(o * o, axis=1, keepdims=True) + NORM_EPS)
    return o * inv * gain * (1.0 - LAM_INIT)


def _first_step():
    return (pl.program_id(0) == 0) & (pl.program_id(1) == 0) & (pl.program_id(2) == 0)


def _attn_a_prompt_body(slope_ref, lam_ref, q_ref, k_ref, v_ref, g_ref, wi_ref, o_ref, wo_ref,
                        q_scr, rel_scr, s_scr, m_scr, acc_scr, *, bq):
    h = pl.program_id(1)
    qi = pl.program_id(2)
    slope = slope_ref[h]

    @pl.when(_first_step())
    def _():
        r = lax.broadcasted_iota(I32, (bq, bq), 0)
        c = lax.broadcasted_iota(I32, (bq, bq), 1)
        rel_scr[...] = jnp.where(c // CHUNK <= r // CHUNK, (r - jnp.abs(r - c)).astype(F32), NEG)

    wo_ref[...] = wi_ref[...].astype(wo_ref.dtype)
    q_scr[...] = _stack_components(q_ref[...])
    _init_softmax(m_scr, acc_scr)
    col = lax.broadcasted_iota(I32, (1, bq), 1).astype(F32)
    _flash_sweep(q_scr, k_ref, v_ref, s_scr, m_scr, acc_scr, qi, bq,
                 past_bias=lambda j: slope * (col + ((j - qi) * bq).astype(F32)),
                 diag_bias=slope * rel_scr[...])
    acc = acc_scr[...]
    o1 = acc[:bq, :LANES] / acc[:bq, LANES:]
    o2 = acc[bq:, :LANES] / acc[bq:, LANES:]
    o_ref[...] = _diff_combine(o1, o2, lam_ref[0], g_ref[...]).astype(o_ref.dtype)


def _cast_specs(w, steps, index):
    rows = w.shape[0] // steps
    assert rows * steps == w.shape[0] and rows % 16 == 0
    spec = pl.BlockSpec((rows, w.shape[1]), lambda b, h, i: (index(b, h, i), 0))
    return spec, spec, jax.ShapeDtypeStruct(w.shape, BF16)


def _attn_a_prompt(aq, ak, av, slopes, lam, gain, w_cast, batch, seq):
    bq = min(BQ_A, seq)
    nq = seq // bq
    smem = pl.BlockSpec(memory_space=pltpu.SMEM)
    w_in_spec, w_out_spec, w_shape = _cast_specs(w_cast, batch * H_A * nq, lambda b, h, i: (b * H_A + h) * nq + i)
    return pl.pallas_call(
        functools.partial(_attn_a_prompt_body, bq=bq),
        grid=(batch, H_A, nq),
        in_specs=[smem, smem,
                  pl.BlockSpec((bq, LANES), lambda b, h, i: (b * nq + i, h)),
                  pl.BlockSpec((seq, LANES), lambda b, h, i: (b, h)),
                  pl.BlockSpec((seq, LANES), lambda b, h, i: (b, h)),
                  pl.BlockSpec((1, LANES), lambda b, h, i: (0, 0)),
                  w_in_spec],
        out_specs=(pl.BlockSpec((bq, LANES), lambda b, h, i: (b * nq + i, h)), w_out_spec),
        out_shape=(jax.ShapeDtypeStruct(aq.shape, BF16), w_shape),
        scratch_shapes=[pltpu.VMEM((2 * bq, LANES), BF16), pltpu.VMEM((bq, bq), F32),
                        pltpu.VMEM((2, 2 * bq, bq), F32), pltpu.VMEM((2 * bq, LANES), F32),
                        pltpu.VMEM((2 * bq, 2 * LANES), F32)],
        compiler_params=_params(3),
        name="attn_a_prompt",
    )(slopes, lam, aq, ak, av, gain, w_cast)


def _attn_b_prompt_body(q_ref, k_ref, v_ref, c_ref, wi_ref, o_ref, wo_ref, mask_scr, s_scr, m_scr, acc_scr, *, bq):
    qi = pl.program_id(2)

    @pl.when(_first_step())
    def _():
        r = lax.broadcasted_iota(I32, (bq, bq), 0)
        c = lax.broadcasted_iota(I32, (bq, bq), 1)
        mask_scr[...] = jnp.where(c <= r, 0.0, NEG)

    wo_ref[...] = wi_ref[...].astype(wo_ref.dtype)
    _init_softmax(m_scr, acc_scr)
    c_tile = c_ref[pl.ds(qi, 1), :]
    c_first = c_tile[:, 0:1]
    _flash_sweep(q_ref, k_ref, v_ref, s_scr, m_scr, acc_scr, qi, bq,
                 past_bias=lambda j: c_first - c_ref[pl.ds(j, 1), :],
                 diag_bias=mask_scr[...] + (c_first - c_tile))
    acc = acc_scr[...]
    o_ref[...] = (acc[:, :LANES] / acc[:, LANES:]).astype(o_ref.dtype)


def _attn_b_prompt(bq_arr, bk, bv, c, w_cast, batch, seq):
    bq = min(BQ_B, seq)
    nq = seq // bq
    c4 = c.reshape(batch, H_B, nq, bq)
    w_in_spec, w_out_spec, w_shape = _cast_specs(w_cast, batch * H_B * nq, lambda b, h, i: (b * H_B + h) * nq + i)
    return pl.pallas_call(
        functools.partial(_attn_b_prompt_body, bq=bq),
        grid=(batch, H_B, nq),
        in_specs=[pl.BlockSpec((bq, LANES), lambda b, h, i: (b * nq + i, h)),
                  pl.BlockSpec((seq, LANES), lambda b, h, i: (b, h)),
                  pl.BlockSpec((seq, LANES), lambda b, h, i: (b, h)),
                  pl.BlockSpec((None, None, nq, bq), lambda b, h, i: (b, h, 0, 0)),
                  w_in_spec],
        out_specs=(pl.BlockSpec((bq, LANES), lambda b, h, i: (b * nq + i, h)), w_out_spec),
        out_shape=(jax.ShapeDtypeStruct(bq_arr.shape, BF16), w_shape),
        scratch_shapes=[pltpu.VMEM((bq, bq), F32), pltpu.VMEM((2, bq, bq), F32), pltpu.VMEM((bq, LANES), F32),
                        pltpu.VMEM((bq, 2 * LANES), F32)],
        compiler_params=_params(3),
        name="attn_b_prompt",
    )(bq_arr, bk, bv, c4, w_cast)


def _two_part_softmax(s_c, s_n, v_c, v_n):
    m = jnp.maximum(jnp.max(s_c, axis=1, keepdims=True), jnp.max(s_n, axis=1, keepdims=True))
    p_c = jnp.exp(s_c - m)
    p_n = jnp.exp(s_n - m)
    l = jnp.sum(p_c, axis=1, keepdims=True) + jnp.sum(p_n, axis=1, keepdims=True)
    acc = (jnp.dot(p_c.astype(BF16), v_c, preferred_element_type=F32)
           + jnp.dot(p_n.astype(BF16), v_n, preferred_element_type=F32))
    return acc / l


def _attn_a_sample_body(slope_ref, lam_ref, q_ref, kc_ref, vc_ref, kn_ref, vn_ref, g_ref, o_ref, *, past, sq):
    col_c = lax.broadcasted_iota(I32, (1, past), 1).astype(F32) - float(past)
    r = lax.broadcasted_iota(I32, (2 * sq, sq), 0)
    r = jnp.where(r >= sq, r - sq, r)
    c = lax.broadcasted_iota(I32, (2 * sq, sq), 1)
    rel_n = (r - jnp.abs(r - c)).astype(F32)
    for h in range(H_A):
        hs = slice(h * LANES, (h + 1) * LANES)
        slope = slope_ref[h]
        qq = _stack_components(q_ref[:, hs])
        s_c = _nt_dot(qq, kc_ref[pl.ds(h, past, stride=H_A), :].astype(BF16)) + slope * col_c
        s_n = _nt_dot(qq, kn_ref[:, hs]) + slope * rel_n
        o = _two_part_softmax(s_c, s_n, vc_ref[pl.ds(h, past, stride=H_A), :].astype(BF16), vn_ref[:, hs])
        o_ref[:, hs] = _diff_combine(o[:sq], o[sq:], lam_ref[0], g_ref[...]).astype(o_ref.dtype)


def _attn_b_sample_body(q_ref, kc_ref, vc_ref, kn_ref, vn_ref, c_ref, o_ref, *, past, sq):
    r = lax.broadcasted_iota(I32, (sq, sq), 0)
    c = lax.broadcasted_iota(I32, (sq, sq), 1)
    causal = c <= r
    for h in range(H_B):
        hs = slice(h * LANES, (h + 1) * LANES)
        q = q_ref[:, hs]
        c_row = c_ref[h:h + 1, :]
        c_first = c_row[:, past:past + 1]
        s_c = _nt_dot(q, kc_ref[pl.ds(h, past, stride=H_B), :].astype(BF16)) + (c_first - c_row[:, :past])
        s_n = _nt_dot(q, kn_ref[:, hs]) + (c_first - c_row[:, past:past + sq])
        s_n = jnp.where(causal, s_n, NEG)
        o = _two_part_softmax(s_c, s_n, vc_ref[pl.ds(h, past, stride=H_B), :].astype(BF16), vn_ref[:, hs])
        o_ref[:, hs] = o.astype(o_ref.dtype)


def _attn_sample(body, q, kc, vc, kn, vn, tail, smem_args, batch, past, sq):
    width = q.shape[1]
    smem = pl.BlockSpec(memory_space=pltpu.SMEM)
    cache = pl.BlockSpec((None,) + kc.shape[1:], lambda b: (b, 0, 0))
    in_specs = [smem] * len(smem_args) + [
        pl.BlockSpec((sq, width), lambda b: (b, 0)), cache, cache,
        pl.BlockSpec((sq, width), lambda b: (b, 0)),
        pl.BlockSpec((sq, width), lambda b: (b, 0))]
    if tail.ndim == 3:
        in_specs.append(pl.BlockSpec((None,) + tail.shape[1:], lambda b: (b, 0, 0)))
    else:
        in_specs.append(pl.BlockSpec(tail.shape, lambda b: (0, 0)))
    return pl.pallas_call(
        functools.partial(body, past=past, sq=sq),
        grid=(batch,),
        in_specs=in_specs,
        out_specs=pl.BlockSpec((sq, width), lambda b: (b, 0)),
        out_shape=jax.ShapeDtypeStruct(q.shape, BF16),
        compiler_params=_params(1),
        name="attn_sample",
    )(*smem_args, q, kc, vc, kn, vn, tail)


def _merge_body(x_ref, h_ref, oa_ref, ob_ref, woa_ref, wob_ref, wga_ref, wgb_ref, bga_ref, bgb_ref, wout_ref,
                o_ref, acc_scr):
    j = pl.program_id(1)

    @pl.when(j == 0)
    def _():
        acc_scr[...] = jnp.zeros(acc_scr.shape, F32)

    h = h_ref[...]
    y_a = jnp.dot(oa_ref[...], woa_ref[...], preferred_element_type=F32)
    y_b = jnp.dot(ob_ref[...], wob_ref[...], preferred_element_type=F32)
    g_a = jax.nn.sigmoid(jnp.dot(h, wga_ref[...], preferred_element_type=F32) + bga_ref[...])
    g_b = jax.nn.sigmoid(jnp.dot(h, wgb_ref[...], preferred_element_type=F32) + bgb_ref[...])
    merged = (g_a * y_a + g_b * y_b).astype(BF16)
    acc_scr[...] += jnp.dot(merged, wout_ref[...], preferred_element_type=F32)

    @pl.when(j == pl.num_programs(1) - 1)
    def _():
        o_ref[...] = x_ref[...] + acc_scr[...]


def _merge(x, h, o_a, o_b, w_o_a, w_o_b, w_ga, w_gb, b_ga, b_gb, w_out):
    m, d = x.shape
    tm = min(TM_MERGE, m)
    tn = TN_MERGE
    row = lambda n: pl.BlockSpec((tm, n), lambda i, j: (i, 0))
    colw = lambda k: pl.BlockSpec((k, tn), lambda i, j: (0, j))
    return pl.pallas_call(
        _merge_body,
        grid=(m // tm, d // tn),
        in_specs=[row(d), row(d), row(A_V), row(B_QK), colw(A_V), colw(B_QK), colw(d), colw(d),
                  pl.BlockSpec((1, tn), lambda i, j: (0, j)), pl.BlockSpec((1, tn), lambda i, j: (0, j)),
                  pl.BlockSpec((tn, d), lambda i, j: (j, 0))],
        out_specs=row(d),
        out_shape=jax.ShapeDtypeStruct((m, d), F32),
        scratch_shapes=[pltpu.VMEM((tm, d), F32)],
        compiler_params=_params(2),
        name="merge",
    )(x, h, o_a, o_b, w_o_a, w_o_b, w_ga, w_gb, b_ga, b_gb, w_out)


def _router_body(x_ref, g_ref, whi_ref, wlo_ref, b_ref, cin_ref, sel_ref, gate_ref, cout_ref, cnt_scr):
    i = pl.program_id(0)

    @pl.when(i == 0)
    def _():
        cnt_scr[...] = cin_ref[...]

    x = x_ref[...]
    tm = x.shape[0]
    h = x * lax.rsqrt(jnp.mean(x * x, axis=-1, keepdims=True) + NORM_EPS) * g_ref[...]
    hi = h.astype(BF16)
    lo = (h - hi.astype(F32)).astype(BF16)
    logits = (jnp.dot(hi, whi_ref[...], preferred_element_type=F32)
              + jnp.dot(hi, wlo_ref[...], preferred_element_type=F32)
              + jnp.dot(lo, whi_ref[...], preferred_element_type=F32)) + b_ref[...]

    lane = lax.broadcasted_iota(I32, logits.shape, 1)
    work = logits
    picks, vals, ids = [], [], []
    for _ in range(TOP_K):
        mx = jnp.max(work, axis=1, keepdims=True)
        idx = jnp.min(jnp.where(work == mx, lane, LANES), axis=1, keepdims=True)
        pick = lane == idx
        work = jnp.where(pick, -jnp.inf, work)
        picks.append(pick)
        vals.append(mx)
        ids.append(idx)

    exps = [jnp.exp(v - vals[0]) for v in vals]
    denom = exps[0] + exps[1] + exps[2] + exps[3]

    picked = jnp.zeros(logits.shape, F32)
    for pick in picks:
        picked = picked + jnp.where(pick, 1.0, 0.0)
    r = lax.broadcasted_iota(I32, (tm, tm), 0)
    c = lax.broadcasted_iota(I32, (tm, tm), 1)
    earlier = jnp.where(c < r, 1.0, 0.0).astype(BF16)
    before = jnp.dot(earlier, picked.astype(BF16), preferred_element_type=F32) + cnt_scr[...]

    sel = jnp.zeros(logits.shape, I32)
    gate = jnp.zeros(logits.shape, F32)
    for k in range(TOP_K):
        rank = jnp.sum(jnp.where(picks[k], before, 0.0), axis=1, keepdims=True).astype(I32)
        sel = jnp.where(lane == k, ids[k], sel)
        sel = jnp.where(lane == TOP_K + k, rank, sel)
        gate = jnp.where(lane == k, exps[k] / denom, gate)
    sel_ref[...] = sel
    gate_ref[...] = gate
    cnt_scr[...] += jnp.sum(picked, axis=0, keepdims=True)
    cout_ref[...] = cnt_scr[...]


def _router(x, g, w_hi, w_lo, b, counts_in):
    m, d = x.shape
    tm = min(TM_ROUTE, m)
    const = lambda shape: pl.BlockSpec(shape, lambda i: (0, 0))
    return pl.pallas_call(
        _router_body,
        grid=(m // tm,),
        in_specs=[pl.BlockSpec((tm, d), lambda i: (i, 0)), const((1, d)), const((d, LANES)), const((d, LANES)),
                  const((1, LANES)), const((1, LANES))],
        out_specs=(pl.BlockSpec((tm, LANES), lambda i: (i, 0)), pl.BlockSpec((tm, LANES), lambda i: (i, 0)),
                   const((1, LANES))),
        out_shape=(jax.ShapeDtypeStruct((m, LANES), I32), jax.ShapeDtypeStruct((m, LANES), F32),
                   jax.ShapeDtypeStruct((1, LANES), F32)),
        scratch_shapes=[pltpu.VMEM((1, LANES), F32)],
        compiler_params=_params(1),
        name="router",
    )(x, g, w_hi, w_lo, b, counts_in)


def _dispatch_body(fill_ref, dest_ref, g_ref, xp_ref, xs_ref, o_ref, zero_scr, row_scr, sem, fill_sem, *, tt,
                   tiles_p, tiles, tm, n_blocks):
    i = pl.program_id(0)
    slot = i % 2

    @pl.when(i == 0)
    def _():
        zero_scr[...] = jnp.zeros(zero_scr.shape, F32)
        for e in range(N_EXPERTS):
            @pl.when(fill_ref[e] >= 0)
            def _():
                off = pl.multiple_of(fill_ref[e], tm)
                pltpu.make_async_copy(zero_scr, o_ref.at[pl.ds(off, tm)], fill_sem).start()
        for e in range(N_EXPERTS):
            @pl.when(fill_ref[e] >= 0)
            def _():
                pltpu.make_async_copy(zero_scr, o_ref.at[pl.ds(0, tm)], fill_sem).wait()

        def zero_block(b, carry):
            off = pl.multiple_of(b * tm, tm)
            pltpu.make_async_copy(zero_scr, o_ref.at[pl.ds(off, tm)], fill_sem).start()
            return carry

        def wait_block(b, carry):
            pltpu.make_async_copy(zero_scr, o_ref.at[pl.ds(0, tm)], fill_sem).wait()
            return carry

        lax.fori_loop(fill_ref[N_EXPERTS], n_blocks, zero_block, 0)
        lax.fori_loop(fill_ref[N_EXPERTS], n_blocks, wait_block, 0)

    def stage(src_ref):
        x = src_ref[...]
        inv = lax.rsqrt(jnp.mean(x * x, axis=-1, keepdims=True) + NORM_EPS)
        row_scr[slot] = x * inv * g_ref[...]

    @pl.when(i < tiles_p)
    def _():
        stage(xp_ref)

    @pl.when(i >= tiles_p)
    def _():
        stage(xs_ref)

    def wait_tile(s):
        for k in range(TOP_K):
            pltpu.make_async_copy(row_scr.at[s], o_ref.at[pl.ds(0, tt)], sem.at[s]).wait()

    def scatter(s):
        def issue(t, carry):
            for k in range(TOP_K):
                d = dest_ref[t * TOP_K + k]
                pltpu.make_async_copy(row_scr.at[s, pl.ds(t, 1)], o_ref.at[pl.ds(d, 1)], sem.at[s]).start()
            return carry

        lax.fori_loop(0, tt, issue, 0)

    for s in range(2):
        pl.when(slot == s)(functools.partial(scatter, s))

    @pl.when(i > 0)
    def _():
        wait_tile(1 - slot)

    @pl.when(i == tiles - 1)
    def _():
        wait_tile(slot)


def _dispatch(fill, dest_flat, g, x_p, x_s, rows, tm):
    m_p, d = x_p.shape
    m_s = x_s.shape[0]
    tt = TT_MOVE
    tiles_p = m_p // tt
    tiles = tiles_p + m_s // tt
    return pl.pallas_call(
        functools.partial(_dispatch_body, tt=tt, tiles_p=tiles_p, tiles=tiles, tm=tm, n_blocks=rows // tm),
        grid=(tiles,),
        in_specs=[pl.BlockSpec(memory_space=pltpu.SMEM),
                  pl.BlockSpec((tt * TOP_K,), lambda i: (i,), memory_space=pltpu.SMEM),
                  pl.BlockSpec((1, d), lambda i: (0, 0)),
                  pl.BlockSpec((tt, d), lambda i: (jnp.minimum(i, tiles_p - 1), 0)),
                  pl.BlockSpec((tt, d), lambda i: (jnp.maximum(i - tiles_p, 0), 0))],
        out_specs=pl.BlockSpec(memory_space=pl.ANY),
        out_shape=jax.ShapeDtypeStruct((rows, d), F32),
        scratch_shapes=[pltpu.VMEM((tm, d), F32), pltpu.VMEM((2, tt, d), F32), pltpu.SemaphoreType.DMA((2,)),
                        pltpu.SemaphoreType.DMA],
        compiler_params=pltpu.CompilerParams(dimension_semantics=("arbitrary",), vmem_limit_bytes=VMEM_LIMIT,
                                             has_side_effects=True),
        name="dispatch",
    )(fill, dest_flat, g, x_p, x_s)


def _experts_body(blk_ref, used_ref, x_ref, wg_ref, wu_ref, bg_ref, bu_ref, wd_ref, bd_ref, o_ref, h_scr, acc_scr):
    b = pl.program_id(0)
    f = pl.program_id(1)

    @pl.when(b < used_ref[0])
    def _():
        @pl.when(f == 0)
        def _():
            h_scr[...] = x_ref[...].astype(BF16)
            acc_scr[...] = jnp.zeros(acc_scr.shape, F32)

        h = h_scr[...]
        g = jnp.dot(h, wg_ref[...], preferred_element_type=F32) + bg_ref[...]
        u = jnp.dot(h, wu_ref[...], preferred_element_type=F32) + bu_ref[...]
        g = jnp.minimum(g, SWIGLU_LIMIT)
        u = jnp.clip(u, -SWIGLU_LIMIT, SWIGLU_LIMIT)
        act = (u + 1.0) * g * jax.nn.sigmoid(SWIGLU_ALPHA * g)
        acc_scr[...] += jnp.dot(act.astype(BF16), wd_ref[...], preferred_element_type=F32)

        @pl.when(f == pl.num_programs(1) - 1)
        def _():
            o_ref[...] = acc_scr[...] + bd_ref[...]

    @pl.when((b >= used_ref[0]) & (f == 0))
    def _():
        o_ref[...] = jnp.zeros(o_ref.shape, F32)


def _experts(blk_e, n_used, xs, w_gu, b_gu, w_d, b_d, tm):
    rows, d = xs.shape
    tf = TF_EXP
    nf = D_FF // tf
    n_blocks = rows // tm

    def blk(b, used):
        return jnp.minimum(b, used[0] - 1)

    def fch(b, f, used):
        return jnp.where(b < used[0], f, nf - 1)

    grid_spec = pltpu.PrefetchScalarGridSpec(
        num_scalar_prefetch=2,
        grid=(n_blocks, nf),
        in_specs=[
            pl.BlockSpec((tm, d), lambda b, f, e, u: (blk(b, u), 0)),
            pl.BlockSpec((None, d, tf), lambda b, f, e, u: (e[blk(b, u)], 0, fch(b, f, u))),
            pl.BlockSpec((None, d, tf), lambda b, f, e, u: (e[blk(b, u)], 0, nf + fch(b, f, u))),
            pl.BlockSpec((None, 1, tf), lambda b, f, e, u: (e[blk(b, u)], 0, fch(b, f, u))),
            pl.BlockSpec((None, 1, tf), lambda b, f, e, u: (e[blk(b, u)], 0, nf + fch(b, f, u))),
            pl.BlockSpec((None, tf, d), lambda b, f, e, u: (e[blk(b, u)], fch(b, f, u), 0)),
            pl.BlockSpec((None, 1, d), lambda b, f, e, u: (e[blk(b, u)], 0, 0)),
        ],
        out_specs=pl.BlockSpec((tm, d), lambda b, f, e, u: (b, 0)),
        scratch_shapes=[pltpu.VMEM((tm, d), BF16), pltpu.VMEM((tm, d), F32)],
    )
    return pl.pallas_call(
        _experts_body,
        grid_spec=grid_spec,
        out_shape=jax.ShapeDtypeStruct((rows, d), F32),
        compiler_params=_params(2),
        name="experts",
    )(blk_e, n_used, xs, w_gu, w_gu, b_gu, b_gu, w_d, b_d)


def _combine_body(dest_ref, next_ref, x_ref, gate_ref, ys_ref, o_ref, buf, sem, *, tt, tiles):
    i = pl.program_id(0)
    slot = i % 2

    def gather(d_ref, s):
        def issue(t, carry):
            for k in range(TOP_K):
                d = d_ref[t * TOP_K + k]
                pltpu.make_async_copy(ys_ref.at[pl.ds(d, 1)], buf.at[s, k, pl.ds(t, 1)], sem.at[s]).start()
            return carry

        lax.fori_loop(0, tt, issue, 0)

    @pl.when(i == 0)
    def _():
        gather(dest_ref, 0)

    for s in range(2):
        pl.when((i + 1 < tiles) & (slot == 1 - s))(functools.partial(gather, next_ref, s))

    for k in range(TOP_K):
        pltpu.make_async_copy(ys_ref.at[pl.ds(0, tt)], buf.at[slot, k], sem.at[slot]).wait()

    gate = gate_ref[...]
    out = x_ref[...]
    for k in range(TOP_K):
        out = out + gate[:, k:k + 1] * buf[slot, k]
    o_ref[...] = out


def _combine(dest_flat, x, gate, ys):
    m, d = x.shape
    tt = TT_MOVE
    tiles = m // tt
    return pl.pallas_call(
        functools.partial(_combine_body, tt=tt, tiles=tiles),
        grid=(tiles,),
        in_specs=[pl.BlockSpec((tt * TOP_K,), lambda i: (i,), memory_space=pltpu.SMEM),
                  pl.BlockSpec((tt * TOP_K,), lambda i: (jnp.minimum(i + 1, tiles - 1),), memory_space=pltpu.SMEM),
                  pl.BlockSpec((tt, d), lambda i: (i, 0)),
                  pl.BlockSpec((tt, LANES), lambda i: (i, 0)),
                  pl.BlockSpec(memory_space=pl.ANY)],
        out_specs=pl.BlockSpec((tt, d), lambda i: (i, 0)),
        out_shape=jax.ShapeDtypeStruct((m, d), F32),
        scratch_shapes=[pltpu.VMEM((2, TOP_K, tt, d), F32), pltpu.SemaphoreType.DMA((2,))],
        compiler_params=_params(1),
        name="combine",
    )(dest_flat, dest_flat, x, gate, ys)


def _moe(x_p, x_s, g_ffn, w_router, b_router, w_gate_up, b_gate_up, w_down, b_down):
    d = x_p.shape[1]
    tm = TM_EXP
    g = g_ffn.astype(F32).reshape(1, d)
    w_r = jnp.pad(w_router.astype(F32), ((0, 0), (0, LANES - N_EXPERTS)))
    w_hi = w_r.astype(BF16)
    w_lo = (w_r - w_hi.astype(F32)).astype(BF16)
    b_r = jnp.pad(b_router.astype(F32), (0, LANES - N_EXPERTS), constant_values=NEG).reshape(1, LANES)

    sel_p, gate_p, counts = _router(x_p, g, w_hi, w_lo, b_r, jnp.zeros((1, LANES), F32))
    sel_s, gate_s, counts = _router(x_s, g, w_hi, w_lo, b_r, counts)

    counts = counts[0, :N_EXPERTS].astype(I32)
    padded = (counts + tm - 1) // tm * tm
    pad_end = jnp.cumsum(padded)
    pad_start = pad_end - padded
    n_assign = (x_p.shape[0] + x_s.shape[0]) * TOP_K
    n_blocks = -(-n_assign // tm) + N_EXPERTS
    rows = n_blocks * tm
    blk_start = jnp.arange(n_blocks, dtype=I32) * tm
    blk_e = jnp.minimum(jnp.sum(pad_end[None, :] <= blk_start[:, None], axis=1), N_EXPERTS - 1).astype(I32)
    n_used = (pad_end[-1:] // tm).astype(I32)
    fill = jnp.concatenate([jnp.where(counts > 0, pad_end - tm, -1), n_used]).astype(I32)

    def dest_of(sel):
        return (pad_start[sel[:, :TOP_K]] + sel[:, TOP_K:2 * TOP_K]).reshape(-1)

    dest_p, dest_s = dest_of(sel_p), dest_of(sel_s)
    xs = _dispatch(fill, jnp.concatenate([dest_p, dest_s]), g, x_p, x_s, rows, tm)
    ys = _experts(blk_e, n_used, xs, w_gate_up, b_gate_up.astype(F32)[:, None, :],
                  w_down, b_down.astype(F32)[:, None, :], tm)
    return _combine(dest_p, x_p, gate_p, ys), _combine(dest_s, x_s, gate_s, ys)


def kernel(x_prompt, x_sample, cache_a_k, cache_a_v, cache_b_k, cache_b_v, cache_b_logf, g_attn, w_in, b_gate,
           a_q_norm, a_k_norm, b_q_norm, b_k_norm, b_f, lambda_q1, lambda_k1, lambda_q2, lambda_k2, a_subln,
           w_o_a, w_o_b, w_out, g_ffn, w_router, b_router, w_gate_up, b_gate_up, w_down, b_down):
    depth = g_attn.shape[0]
    assert depth == 1, "single-layer trunk"
    batch, seq, d = x_prompt.shape
    dec_batch, dec_seq, _ = x_sample.shape
    past = cache_a_k.shape[2]
    assert dec_seq == CHUNK and past % CHUNK == 0, "the new sample frames must form exactly one chunk"
    sk = past + dec_seq
    sk_pad = -(-sk // LANES) * LANES

    xp = x_prompt.reshape(batch * seq, d)
    xs = x_sample.reshape(dec_batch * dec_seq, d)
    g_a = g_attn[0].astype(F32)
    w_in0 = w_in[0]
    split = 2 * A_QK + A_V + 3 * B_QK + H_B
    w_ga = w_in0[:, split:split + d].astype(BF16)
    w_gb = w_in0[:, split + d:split + 2 * d].astype(BF16)
    b_ga = b_gate[0, :d].astype(F32).reshape(1, d)
    b_gb = b_gate[0, d:].astype(F32).reshape(1, d)
    w_oa, w_ob, w_o = w_o_a[0].astype(BF16), w_o_b[0].astype(BF16), w_out[0].astype(BF16)

    slopes = jnp.exp2(-8.0 * jnp.arange(1, H_A + 1, dtype=F32) / H_A)
    lam = (jnp.exp(jnp.sum(lambda_q1[0].astype(F32) * lambda_k1[0].astype(F32)))
           - jnp.exp(jnp.sum(lambda_q2[0].astype(F32) * lambda_k2[0].astype(F32))) + LAM_INIT).reshape(1)
    subln = a_subln[0].astype(F32).reshape(1, DV_A)

    norms = (a_q_norm[0], a_k_norm[0], b_q_norm[0], b_k_norm[0], b_f[0])

    hp = _rms_bf16(xp, g_a)
    aq, (pak, ak16), (pav, av16), bq, (pbk, bk16), (pbv, bv16), plogf, plogf_t = _project_qkv(hp, w_in0, *norms)
    cp = _cumsum_lanes(plogf_t.reshape(H_B, batch, seq).transpose(1, 0, 2))
    n_e, _, two_f = w_gate_up[0].shape
    oa_p, w_gu16 = _attn_a_prompt(aq, ak16, av16, slopes, lam, subln, w_gate_up[0].reshape(n_e * d, two_f),
                                  batch, seq)
    ob_p, w_d16 = _attn_b_prompt(bq, bk16, bv16, cp, w_down[0].reshape(n_e * D_FF, d), batch, seq)
    x2_p = _merge(xp, hp, oa_p, ob_p, w_oa, w_ob, w_ga, w_gb, b_ga, b_gb, w_o)

    hs = _rms_bf16(xs, g_a)
    aq, (sak, ak16), (sav, av16), bq, (sbk, bk16), (sbv, bv16), slogf, slogf_t = _project_qkv(hs, w_in0, *norms)
    logf_all = jnp.concatenate(
        [cache_b_logf[0].astype(F32).transpose(0, 2, 1),
         slogf_t.reshape(H_B, dec_batch, dec_seq).transpose(1, 0, 2),
         jnp.zeros((dec_batch, H_B, sk_pad - sk), F32)], axis=2)
    cs = _cumsum_lanes(logf_all)
    rows_of = lambda cache: cache[0].reshape(dec_batch, past * cache.shape[3], LANES)
    oa_s = _attn_sample(_attn_a_sample_body, aq, rows_of(cache_a_k), rows_of(cache_a_v), ak16, av16, subln,
                        (slopes, lam), dec_batch, past, dec_seq)
    ob_s = _attn_sample(_attn_b_sample_body, bq, rows_of(cache_b_k), rows_of(cache_b_v), bk16, bv16, cs, (),
                        dec_batch, past, dec_seq)
    x2_s = _merge(xs, hs, oa_s, ob_s, w_oa, w_ob, w_ga, w_gb, b_ga, b_gb, w_o)

    y_p, y_s = _moe(x2_p, x2_s, g_ffn[0], w_router[0], b_router[0], w_gu16.reshape(n_e, d, two_f), b_gate_up[0],
                    w_d16.reshape(n_e, D_FF, d), b_down[0])

    def heads(a, b, s, h):
        return a.reshape(1, b, s, h, -1)

    return (y_p.reshape(batch, seq, d), y_s.reshape(dec_batch, dec_seq, d),
            heads(pak, batch, seq, H_A), heads(pav, batch, seq, H_A), heads(pbk, batch, seq, H_B),
            heads(pbv, batch, seq, H_B), plogf.reshape(1, batch, seq, H_B),
            heads(sak, dec_batch, dec_seq, H_A), heads(sav, dec_batch, dec_seq, H_A),
            heads(sbk, dec_batch, dec_seq, H_B), heads(sbv, dec_batch, dec_seq, H_B),
            slogf.reshape(1, dec_batch, dec_seq, H_B))
```

```python
import functools
import math

import jax
import jax.numpy as jnp
from jax import lax
from jax.experimental import pallas as pl
from jax.experimental.pallas import tpu as pltpu

F32 = jnp.float32
BF16 = jnp.bfloat16
I32 = jnp.int32

D_MODEL = 2048
CHUNK = 64
H_A, DA, DV_A = 8, 64, 128
H_B, DB = 8, 128
A_QK, A_V, B_QK = H_A * 2 * DA, H_A * DV_A, H_B * DB
N_EXPERTS, TOP_K, D_FF = 32, 4, 2048
SWIGLU_LIMIT, SWIGLU_ALPHA = 7.0, 1.702
NORM_EPS = 1e-6
LAM_INIT = 0.8 - 0.6 * math.exp(-0.3 * 0)

LANES = 128
VMEM_LIMIT = 56 * 1024 * 1024
NEG = -1e30

TM_PROJ = 1024
RB_PROJ = 256
BK_ATTN = 512
SUB_ATTN = 2
TM_MERGE = 512
TN_MERGE = 512
TM_ROUTE = 512
TT_MOVE = 256
TM_EXP = 512
TF_EXP = 1024


def _params(n_axes):
    return pltpu.CompilerParams(dimension_semantics=("arbitrary",) * n_axes, vmem_limit_bytes=VMEM_LIMIT)


def _row_chunks(n_rows):
    rb = min(RB_PROJ, n_rows)
    return [slice(r0, r0 + rb) for r0 in range(0, n_rows, rb)]


def _group_rms(y, gain, group):
    lane = lax.broadcasted_iota(I32, (y.shape[0], LANES), 1)
    for s in range(y.shape[1] // LANES):
        cols = slice(s * LANES, (s + 1) * LANES)
        slab = y[:, cols]
        sq = slab * slab
        if group == LANES:
            inv = lax.rsqrt(jnp.mean(sq, axis=1, keepdims=True) + NORM_EPS)
        else:
            lo = lane < group
            s_lo = jnp.sum(jnp.where(lo, sq, 0.0), axis=1, keepdims=True)
            s_hi = jnp.sum(jnp.where(lo, 0.0, sq), axis=1, keepdims=True)
            inv = jnp.where(lo, lax.rsqrt(s_lo / group + NORM_EPS), lax.rsqrt(s_hi / group + NORM_EPS))
        yield cols, slab * inv * gain


def _proj_plain_body(h_ref, w_ref, *o_refs):
    for rows in _row_chunks(h_ref.shape[0]):
        y = jnp.dot(h_ref[rows, :], w_ref[...], preferred_element_type=F32)
        for o_ref in o_refs:
            o_ref[rows, :] = y.astype(o_ref.dtype)


def _proj_norm_body(h_ref, w_ref, g_ref, *o_refs, group, scale):
    gain = g_ref[...] * scale
    for rows in _row_chunks(h_ref.shape[0]):
        y = jnp.dot(h_ref[rows, :], w_ref[...], preferred_element_type=F32)
        for cols, val in _group_rms(y, gain, group):
            for o_ref in o_refs:
                o_ref[rows, cols] = val.astype(o_ref.dtype)


def _proj_first_body(x_ref, gx_ref, w_ref, g_ref, h_ref, o_ref, *, group, scale):
    gain = g_ref[...] * scale
    for rows in _row_chunks(x_ref.shape[0]):
        x = x_ref[rows, :]
        ms = jnp.mean(x * x, axis=-1, keepdims=True)
        h = (x * lax.rsqrt(ms + NORM_EPS) * gx_ref[...]).astype(BF16)
        h_ref[rows, :] = h
        y = jnp.dot(h, w_ref[...], preferred_element_type=F32)
        for cols, val in _group_rms(y, gain, group):
            o_ref[rows, cols] = val.astype(o_ref.dtype)


def _proj_logf_body(h_ref, w_ref, b_ref, o_ref, ot_ref):
    z = jnp.dot(h_ref[...], w_ref[...], preferred_element_type=F32) + b_ref[...]
    logf = jnp.minimum(z, 0.0) - jnp.log1p(jnp.exp(-jnp.abs(z)))
    o_ref[...] = logf[:, :H_B]
    ot_ref[...] = logf.T[:H_B, :]


def _proj(body, h, w, extra, out_shapes, out_specs, pre=()):
    m, d = h.shape
    tm = min(TM_PROJ, m)
    const = lambda a: pl.BlockSpec(a.shape, lambda i: (0, 0))
    in_specs = [pl.BlockSpec((tm, d), lambda i: (i, 0))] + [const(a) for a in (*pre, w, *extra)]
    return pl.pallas_call(
        body,
        grid=(m // tm,),
        in_specs=in_specs,
        out_specs=out_specs(tm),
        out_shape=out_shapes,
        compiler_params=_params(1),
        name="proj",
    )(h, *pre, w, *extra)


def _project_qkv(x, g_x, w_in, a_q_norm, a_k_norm, b_q_norm, b_k_norm, b_f):
    m, d = x.shape
    w = w_in.astype(BF16)
    o = 0
    cols = {}
    for name, width in (("aq", A_QK), ("ak", A_QK), ("av", A_V), ("bq", B_QK), ("bk", B_QK), ("bv", B_QK)):
        cols[name] = w[:, o:o + width]
        o += width
    w_f = jnp.pad(w[:, o:o + H_B], ((0, 0), (0, LANES - H_B)))
    b_f_pad = jnp.pad(b_f.astype(F32), (0, LANES - H_B)).reshape(1, LANES)
    tiled = lambda gain, group: jnp.tile(gain.astype(F32), LANES // group).reshape(1, LANES)

    def shapes_specs(n, dtypes):
        shapes = tuple(jax.ShapeDtypeStruct((m, n), dt) for dt in dtypes)
        specs = lambda tm: tuple(pl.BlockSpec((tm, n), lambda i: (i, 0)) for _ in dtypes)
        return shapes, specs

    def run(body, name, extra, dtypes):
        return _proj(body, h, cols[name], extra, *shapes_specs(cols[name].shape[1], dtypes))

    def normed(name, gain, group, scale, dtypes):
        return run(functools.partial(_proj_norm_body, group=group, scale=scale), name, [tiled(gain, group)], dtypes)

    first_shapes = (jax.ShapeDtypeStruct((m, d), BF16), jax.ShapeDtypeStruct((m, A_QK), BF16))
    first_specs = lambda tm: (pl.BlockSpec((tm, d), lambda i: (i, 0)), pl.BlockSpec((tm, A_QK), lambda i: (i, 0)))
    h, aq = _proj(functools.partial(_proj_first_body, group=DA, scale=DA ** -0.5), x, cols["aq"],
                  [tiled(a_q_norm, DA)], first_shapes, first_specs, pre=(g_x.astype(F32).reshape(1, d),))
    ak = normed("ak", a_k_norm, DA, 1.0, (F32, BF16))
    av = run(_proj_plain_body, "av", [], (F32, BF16))
    (bq,) = normed("bq", b_q_norm, DB, DB ** -0.5, (BF16,))
    bk = normed("bk", b_k_norm, DB, 1.0, (F32, BF16))
    bv = run(_proj_plain_body, "bv", [], (F32, BF16))
    logf, logf_t = _proj(
        _proj_logf_body, h, w_f, [b_f_pad],
        (jax.ShapeDtypeStruct((m, H_B), F32), jax.ShapeDtypeStruct((H_B, m), F32)),
        lambda tm: (pl.BlockSpec((tm, H_B), lambda i: (i, 0)), pl.BlockSpec((H_B, tm), lambda i: (0, i))))
    return h, aq, ak, av, bq, bk, bv, logf, logf_t


def _cumsum_body(x_ref, o_ref):
    x = x_ref[...]
    n = x.shape[1]
    lane = lax.broadcasted_iota(I32, x.shape, 1)
    shift = 1
    while shift < n:
        x = x + jnp.where(lane >= shift, pltpu.roll(x, shift, 1), 0.0)
        shift *= 2
    o_ref[...] = x


def _cumsum_lanes(x):
    b, h, s = x.shape
    return pl.pallas_call(
        _cumsum_body,
        grid=(b,),
        in_specs=[pl.BlockSpec((None, h, s), lambda i: (i, 0, 0))],
        out_specs=pl.BlockSpec((None, h, s), lambda i: (i, 0, 0)),
        out_shape=jax.ShapeDtypeStruct((b, h, s), F32),
        compiler_params=_params(1),
        name="cumsum",
    )(x)


def _nt_dot(a, b):
    return lax.dot_general(a, b, (((1,), (1,)), ((), ())), preferred_element_type=F32)


def _with_ones(v):
    return jnp.concatenate([v, jnp.ones_like(v)], axis=1)


def _flash_sweep(q, k_ref, v_ref, s_scr, m_scr, acc_scr, qi, bk, past_bias, diag_bias):
    n_chunks = q.shape[0] // bk
    buf_a, buf_b = s_scr.at[0], s_scr.at[1]

    def scores(tile, buf, first_sub=0):
        k0 = pl.multiple_of(tile * bk, bk)
        k = k_ref[pl.ds(k0, bk), :]
        for c in range(n_chunks):
            if c % SUB_ATTN >= first_sub:
                buf[c * bk:(c + 1) * bk, :] = _nt_dot(q[c * bk:(c + 1) * bk, :], k)

    def consume(tile, buf, bias_of_sub):
        k0 = pl.multiple_of(tile * bk, bk)
        v_aug = _with_ones(v_ref[pl.ds(k0, bk), :])
        for c in range(n_chunks):
            bias = bias_of_sub(c % SUB_ATTN)
            if bias is None:
                continue
            rows = slice(c * bk, (c + 1) * bk)
            s = buf[rows, :] + bias
            m_prev = m_scr[rows, :]
            m_new = jnp.maximum(m_prev, jnp.max(s, axis=1, keepdims=True))
            alpha = jnp.exp(m_prev - m_new)
            p = jnp.exp(s - jnp.tile(m_new, (1, bk // LANES)))
            acc_scr[rows, :] = (jnp.tile(alpha, (1, 2)) * acc_scr[rows, :]
                                + jnp.dot(p.astype(BF16), v_aug, preferred_element_type=F32))
            m_scr[rows, :] = m_new

    scores(0, buf_a)

    def tile_pair(t, carry):
        j = 2 * t
        scores(j + 1, buf_b)
        consume(j, buf_a, lambda sub: past_bias(j))
        scores(j + 2, buf_a)
        consume(j + 1, buf_b, lambda sub: past_bias(j + 1))
        return carry

    assert SUB_ATTN == 2, "the pair loop and the two-tile tail below are written for two key tiles per query tile"
    lax.fori_loop(0, qi, tile_pair, 0)
    first = SUB_ATTN * qi
    scores(first + 1, buf_b, first_sub=1)
    consume(first, buf_a, lambda sub: diag_bias(0) if sub == 0 else past_bias(first))
    consume(first + 1, buf_b, lambda sub: diag_bias(1) if sub == 1 else None)


def _init_softmax(m_scr, acc_scr):
    m_scr[...] = jnp.full(m_scr.shape, NEG, F32)
    acc_scr[...] = jnp.zeros(acc_scr.shape, F32)


def _stack_components(q):
    lane = lax.broadcasted_iota(I32, q.shape, 1)
    zero = jnp.zeros_like(q)
    return jnp.concatenate([jnp.where(lane < DA, q, zero), jnp.where(lane >= DA, q, zero)], axis=0)


def _diff_combine(o1, o2, lam, gain):
    o = o1 - lam * o2
    inv = lax.rsqrt(jnp.mean(o * o, axis=1, keepdims=True) + NORM_EPS)
    return o * inv * gain * (1.0 - LAM_INIT)


def _first_step():
    return (pl.program_id(0) == 0) & (pl.program_id(1) == 0) & (pl.program_id(2) == 0)


def _attn_a_prompt_body(slope_ref, lam_ref, q_ref, k_ref, v_ref, g_ref, wi_ref, o_ref, wo_ref,
                        q_scr, rel_scr, s_scr, m_scr, acc_scr, *, bk):
    h = pl.program_id(1)
    qi = pl.program_id(2)
    bq = SUB_ATTN * bk
    slope = slope_ref[h]

    @pl.when(_first_step())
    def _():
        r = lax.broadcasted_iota(I32, (bk, bk), 0)
        c = lax.broadcasted_iota(I32, (bk, bk), 1)
        rel_scr[...] = jnp.where(c // CHUNK <= r // CHUNK, (r - jnp.abs(r - c)).astype(F32), NEG)

    wo_ref[...] = wi_ref[...].astype(wo_ref.dtype)
    q_scr[...] = _stack_components(q_ref[...])
    _init_softmax(m_scr, acc_scr)
    col = lax.broadcasted_iota(I32, (1, bk), 1).astype(F32)
    _flash_sweep(q_scr, k_ref, v_ref, s_scr, m_scr, acc_scr, qi, bk,
                 past_bias=lambda j: slope * (col + (j * bk - qi * bq).astype(F32)),
                 diag_bias=lambda sub: slope * (rel_scr[...] + float(sub * bk)))
    acc = acc_scr[...]
    o1 = acc[:bq, :LANES] / acc[:bq, LANES:]
    o2 = acc[bq:, :LANES] / acc[bq:, LANES:]
    o_ref[...] = _diff_combine(o1, o2, lam_ref[0], g_ref[...]).astype(o_ref.dtype)


def _cast_specs(w, steps, index):
    rows = w.shape[0] // steps
    assert rows * steps == w.shape[0] and rows % 16 == 0
    spec = pl.BlockSpec((rows, w.shape[1]), lambda b, h, i: (index(b, h, i), 0))
    return spec, spec, jax.ShapeDtypeStruct(w.shape, BF16)


def _attn_a_prompt(aq, ak, av, slopes, lam, gain, w_cast, batch, seq):
    bk = min(BK_ATTN, seq // SUB_ATTN)
    bq = SUB_ATTN * bk
    nq = seq // bq
    smem = pl.BlockSpec(memory_space=pltpu.SMEM)
    w_in_spec, w_out_spec, w_shape = _cast_specs(w_cast, batch * H_A * nq, lambda b, h, i: (b * H_A + h) * nq + i)
    return pl.pallas_call(
        functools.partial(_attn_a_prompt_body, bk=bk),
        grid=(batch, H_A, nq),
        in_specs=[smem, smem,
                  pl.BlockSpec((bq, LANES), lambda b, h, i: (b * nq + i, h)),
                  pl.BlockSpec((seq, LANES), lambda b, h, i: (b, h)),
                  pl.BlockSpec((seq, LANES), lambda b, h, i: (b, h)),
                  pl.BlockSpec((1, LANES), lambda b, h, i: (0, 0)),
                  w_in_spec],
        out_specs=(pl.BlockSpec((bq, LANES), lambda b, h, i: (b * nq + i, h)), w_out_spec),
        out_shape=(jax.ShapeDtypeStruct(aq.shape, BF16), w_shape),
        scratch_shapes=[pltpu.VMEM((2 * bq, LANES), BF16), pltpu.VMEM((bk, bk), F32),
                        pltpu.VMEM((2, 2 * bq, bk), F32), pltpu.VMEM((2 * bq, LANES), F32),
                        pltpu.VMEM((2 * bq, 2 * LANES), F32)],
        compiler_params=_params(3),
        name="attn_a_prompt",
    )(slopes, lam, aq, ak, av, gain, w_cast)


def _attn_b_prompt_body(q_ref, k_ref, v_ref, c_ref, wi_ref, o_ref, wo_ref, mask_scr, s_scr, m_scr, acc_scr, *, bk):
    qi = pl.program_id(2)

    @pl.when(_first_step())
    def _():
        r = lax.broadcasted_iota(I32, (bk, bk), 0)
        c = lax.broadcasted_iota(I32, (bk, bk), 1)
        mask_scr[...] = jnp.where(c <= r, 0.0, NEG)

    wo_ref[...] = wi_ref[...].astype(wo_ref.dtype)
    _init_softmax(m_scr, acc_scr)
    first = SUB_ATTN * qi
    c_first = c_ref[pl.ds(first, 1), :][:, 0:1]
    past_bias = lambda j: c_first - c_ref[pl.ds(j, 1), :]
    _flash_sweep(q_ref, k_ref, v_ref, s_scr, m_scr, acc_scr, qi, bk,
                 past_bias=past_bias, diag_bias=lambda sub: mask_scr[...] + past_bias(first + sub))
    acc = acc_scr[...]
    o_ref[...] = (acc[:, :LANES] / acc[:, LANES:]).astype(o_ref.dtype)


def _attn_b_prompt(bq_arr, k_arr, v_arr, c, w_cast, batch, seq):
    bk = min(BK_ATTN, seq // SUB_ATTN)
    bq = SUB_ATTN * bk
    nq = seq // bq
    c4 = c.reshape(batch, H_B, seq // bk, bk)
    w_in_spec, w_out_spec, w_shape = _cast_specs(w_cast, batch * H_B * nq, lambda b, h, i: (b * H_B + h) * nq + i)
    return pl.pallas_call(
        functools.partial(_attn_b_prompt_body, bk=bk),
        grid=(batch, H_B, nq),
        in_specs=[pl.BlockSpec((bq, LANES), lambda b, h, i: (b * nq + i, h)),
                  pl.BlockSpec((seq, LANES), lambda b, h, i: (b, h)),
                  pl.BlockSpec((seq, LANES), lambda b, h, i: (b, h)),
                  pl.BlockSpec((None, None, seq // bk, bk), lambda b, h, i: (b, h, 0, 0)),
                  w_in_spec],
        out_specs=(pl.BlockSpec((bq, LANES), lambda b, h, i: (b * nq + i, h)), w_out_spec),
        out_shape=(jax.ShapeDtypeStruct(bq_arr.shape, BF16), w_shape),
        scratch_shapes=[pltpu.VMEM((bk, bk), F32), pltpu.VMEM((2, bq, bk), F32), pltpu.VMEM((bq, LANES), F32),
                        pltpu.VMEM((bq, 2 * LANES), F32)],
        compiler_params=_params(3),
        name="attn_b_prompt",
    )(bq_arr, k_arr, v_arr, c4, w_cast)


def _two_part_softmax(s_c, s_n, v_c, v_n):
    m = jnp.maximum(jnp.max(s_c, axis=1, keepdims=True), jnp.max(s_n, axis=1, keepdims=True))
    p_c = jnp.exp(s_c - m)
    p_n = jnp.exp(s_n - m)
    l = jnp.sum(p_c, axis=1, keepdims=True) + jnp.sum(p_n, axis=1, keepdims=True)
    acc = (jnp.dot(p_c.astype(BF16), v_c, preferred_element_type=F32)
           + jnp.dot(p_n.astype(BF16), v_n, preferred_element_type=F32))
    return acc / l


def _attn_a_sample_body(slope_ref, lam_ref, q_ref, kc_ref, vc_ref, kn_ref, vn_ref, g_ref, o_ref, *, past, sq):
    col_c = lax.broadcasted_iota(I32, (1, past), 1).astype(F32) - float(past)
    r = lax.broadcasted_iota(I32, (2 * sq, sq), 0)
    r = jnp.where(r >= sq, r - sq, r)
    c = lax.broadcasted_iota(I32, (2 * sq, sq), 1)
    rel_n = (r - jnp.abs(r - c)).astype(F32)
    for h in range(H_A):
        hs = slice(h * LANES, (h + 1) * LANES)
        slope = slope_ref[h]
        qq = _stack_components(q_ref[:, hs])
        s_c = _nt_dot(qq, kc_ref[pl.ds(h, past, stride=H_A), :].astype(BF16)) + slope * col_c
        s_n = _nt_dot(qq, kn_ref[:, hs]) + slope * rel_n
        o = _two_part_softmax(s_c, s_n, vc_ref[pl.ds(h, past, stride=H_A), :].astype(BF16), vn_ref[:, hs])
        o_ref[:, hs] = _diff_combine(o[:sq], o[sq:], lam_ref[0], g_ref[...]).astype(o_ref.dtype)


def _attn_b_sample_body(q_ref, kc_ref, vc_ref, kn_ref, vn_ref, c_ref, o_ref, *, past, sq):
    r = lax.broadcasted_iota(I32, (sq, sq), 0)
    c = lax.broadcasted_iota(I32, (sq, sq), 1)
    causal = c <= r
    for h in range(H_B):
        hs = slice(h * LANES, (h + 1) * LANES)
        q = q_ref[:, hs]
        c_row = c_ref[h:h + 1, :]
        c_first = c_row[:, past:past + 1]
        s_c = _nt_dot(q, kc_ref[pl.ds(h, past, stride=H_B), :].astype(BF16)) + (c_first - c_row[:, :past])
        s_n = _nt_dot(q, kn_ref[:, hs]) + (c_first - c_row[:, past:past + sq])
        s_n = jnp.where(causal, s_n, NEG)
        o = _two_part_softmax(s_c, s_n, vc_ref[pl.ds(h, past, stride=H_B), :].astype(BF16), vn_ref[:, hs])
        o_ref[:, hs] = o.astype(o_ref.dtype)


def _attn_sample(body, q, kc, vc, kn, vn, tail, smem_args, batch, past, sq):
    width = q.shape[1]
    smem = pl.BlockSpec(memory_space=pltpu.SMEM)
    cache = pl.BlockSpec((None,) + kc.shape[1:], lambda b: (b, 0, 0))
    in_specs = [smem] * len(smem_args) + [
        pl.BlockSpec((sq, width), lambda b: (b, 0)), cache, cache,
        pl.BlockSpec((sq, width), lambda b: (b, 0)),
        pl.BlockSpec((sq, width), lambda b: (b, 0))]
    if tail.ndim == 3:
        in_specs.append(pl.BlockSpec((None,) + tail.shape[1:], lambda b: (b, 0, 0)))
    else:
        in_specs.append(pl.BlockSpec(tail.shape, lambda b: (0, 0)))
    return pl.pallas_call(
        functools.partial(body, past=past, sq=sq),
        grid=(batch,),
        in_specs=in_specs,
        out_specs=pl.BlockSpec((sq, width), lambda b: (b, 0)),
        out_shape=jax.ShapeDtypeStruct(q.shape, BF16),
        compiler_params=_params(1),
        name="attn_sample",
    )(*smem_args, q, kc, vc, kn, vn, tail)


def _merge_body(x_ref, h_ref, oa_ref, ob_ref, woa_ref, wob_ref, wga_ref, wgb_ref, bga_ref, bgb_ref, wout_ref,
                o_ref, acc_scr):
    j = pl.program_id(1)

    @pl.when(j == 0)
    def _():
        acc_scr[...] = jnp.zeros(acc_scr.shape, F32)

    h = h_ref[...]
    y_a = jnp.dot(oa_ref[...], woa_ref[...], preferred_element_type=F32)
    y_b = jnp.dot(ob_ref[...], wob_ref[...], preferred_element_type=F32)
    g_a = jax.nn.sigmoid(jnp.dot(h, wga_ref[...], preferred_element_type=F32) + bga_ref[...])
    g_b = jax.nn.sigmoid(jnp.dot(h, wgb_ref[...], preferred_element_type=F32) + bgb_ref[...])
    merged = (g_a * y_a + g_b * y_b).astype(BF16)
    acc_scr[...] += jnp.dot(merged, wout_ref[...], preferred_element_type=F32)

    @pl.when(j == pl.num_programs(1) - 1)
    def _():
        o_ref[...] = x_ref[...] + acc_scr[...]


def _merge(x, h, o_a, o_b, w_o_a, w_o_b, w_ga, w_gb, b_ga, b_gb, w_out):
    m, d = x.shape
    tm = min(TM_MERGE, m)
    tn = TN_MERGE
    row = lambda n: pl.BlockSpec((tm, n), lambda i, j: (i, 0))
    colw = lambda k: pl.BlockSpec((k, tn), lambda i, j: (0, j))
    return pl.pallas_call(
        _merge_body,
        grid=(m // tm, d // tn),
        in_specs=[row(d), row(d), row(A_V), row(B_QK), colw(A_V), colw(B_QK), colw(d), colw(d),
                  pl.BlockSpec((1, tn), lambda i, j: (0, j)), pl.BlockSpec((1, tn), lambda i, j: (0, j)),
                  pl.BlockSpec((tn, d), lambda i, j: (j, 0))],
        out_specs=row(d),
        out_shape=jax.ShapeDtypeStruct((m, d), F32),
        scratch_shapes=[pltpu.VMEM((tm, d), F32)],
        compiler_params=_params(2),
        name="merge",
    )(x, h, o_a, o_b, w_o_a, w_o_b, w_ga, w_gb, b_ga, b_gb, w_out)


def _router_body(x_ref, g_ref, whi_ref, wlo_ref, b_ref, cin_ref, sel_ref, gate_ref, cout_ref, cnt_scr):
    i = pl.program_id(0)

    @pl.when(i == 0)
    def _():
        cnt_scr[...] = cin_ref[...]

    x = x_ref[...]
    tm = x.shape[0]
    h = x * lax.rsqrt(jnp.mean(x * x, axis=-1, keepdims=True) + NORM_EPS) * g_ref[...]
    hi = h.astype(BF16)
    lo = (h - hi.astype(F32)).astype(BF16)
    logits = (jnp.dot(hi, whi_ref[...], preferred_element_type=F32)
              + jnp.dot(hi, wlo_ref[...], preferred_element_type=F32)
              + jnp.dot(lo, whi_ref[...], preferred_element_type=F32)) + b_ref[...]

    lane = lax.broadcasted_iota(I32, logits.shape, 1)
    work = logits
    picks, vals, ids = [], [], []
    for _ in range(TOP_K):
        mx = jnp.max(work, axis=1, keepdims=True)
        idx = jnp.min(jnp.where(work == mx, lane, LANES), axis=1, keepdims=True)
        pick = lane == idx
        work = jnp.where(pick, -jnp.inf, work)
        picks.append(pick)
        vals.append(mx)
        ids.append(idx)

    exps = [jnp.exp(v - vals[0]) for v in vals]
    denom = exps[0] + exps[1] + exps[2] + exps[3]

    picked = jnp.zeros(logits.shape, F32)
    for pick in picks:
        picked = picked + jnp.where(pick, 1.0, 0.0)
    r = lax.broadcasted_iota(I32, (tm, tm), 0)
    c = lax.broadcasted_iota(I32, (tm, tm), 1)
    earlier = jnp.where(c < r, 1.0, 0.0).astype(BF16)
    before = jnp.dot(earlier, picked.astype(BF16), preferred_element_type=F32) + cnt_scr[...]

    sel = jnp.zeros(logits.shape, I32)
    gate = jnp.zeros(logits.shape, F32)
    for k in range(TOP_K):
        rank = jnp.sum(jnp.where(picks[k], before, 0.0), axis=1, keepdims=True).astype(I32)
        sel = jnp.where(lane == k, ids[k], sel)
        sel = jnp.where(lane == TOP_K + k, rank, sel)
        gate = jnp.where(lane == k, exps[k] / denom, gate)
    sel_ref[...] = sel
    gate_ref[...] = gate
    cnt_scr[...] += jnp.sum(picked, axis=0, keepdims=True)
    cout_ref[...] = cnt_scr[...]


def _router(x, g, w_hi, w_lo, b, counts_in):
    m, d = x.shape
    tm = min(TM_ROUTE, m)
    const = lambda shape: pl.BlockSpec(shape, lambda i: (0, 0))
    return pl.pallas_call(
        _router_body,
        grid=(m // tm,),
        in_specs=[pl.BlockSpec((tm, d), lambda i: (i, 0)), const((1, d)), const((d, LANES)), const((d, LANES)),
                  const((1, LANES)), const((1, LANES))],
        out_specs=(pl.BlockSpec((tm, LANES), lambda i: (i, 0)), pl.BlockSpec((tm, LANES), lambda i: (i, 0)),
                   const((1, LANES))),
        out_shape=(jax.ShapeDtypeStruct((m, LANES), I32), jax.ShapeDtypeStruct((m, LANES), F32),
                   jax.ShapeDtypeStruct((1, LANES), F32)),
        scratch_shapes=[pltpu.VMEM((1, LANES), F32)],
        compiler_params=_params(1),
        name="router",
    )(x, g, w_hi, w_lo, b, counts_in)


def _dispatch_body(fill_ref, dest_ref, g_ref, xp_ref, xs_ref, o_ref, zero_scr, row_scr, sem, fill_sem, *, tt,
                   tiles_p, tiles, tm, n_blocks):
    i = pl.program_id(0)
    slot = i % 2

    @pl.when(i == 0)
    def _():
        zero_scr[...] = jnp.zeros(zero_scr.shape, F32)
        for e in range(N_EXPERTS):
            @pl.when(fill_ref[e] >= 0)
            def _():
                off = pl.multiple_of(fill_ref[e], tm)
                pltpu.make_async_copy(zero_scr, o_ref.at[pl.ds(off, tm)], fill_sem).start()
        for e in range(N_EXPERTS):
            @pl.when(fill_ref[e] >= 0)
            def _():
                pltpu.make_async_copy(zero_scr, o_ref.at[pl.ds(0, tm)], fill_sem).wait()

        def zero_block(b, carry):
            off = pl.multiple_of(b * tm, tm)
            pltpu.make_async_copy(zero_scr, o_ref.at[pl.ds(off, tm)], fill_sem).start()
            return carry

        def wait_block(b, carry):
            pltpu.make_async_copy(zero_scr, o_ref.at[pl.ds(0, tm)], fill_sem).wait()
            return carry

        lax.fori_loop(fill_ref[N_EXPERTS], n_blocks, zero_block, 0)
        lax.fori_loop(fill_ref[N_EXPERTS], n_blocks, wait_block, 0)

    def stage(src_ref):
        x = src_ref[...]
        inv = lax.rsqrt(jnp.mean(x * x, axis=-1, keepdims=True) + NORM_EPS)
        row_scr[slot] = x * inv * g_ref[...]

    @pl.when(i < tiles_p)
    def _():
        stage(xp_ref)

    @pl.when(i >= tiles_p)
    def _():
        stage(xs_ref)

    def wait_tile(s):
        for k in range(TOP_K):
            pltpu.make_async_copy(row_scr.at[s], o_ref.at[pl.ds(0, tt)], sem.at[s]).wait()

    def scatter(s):
        def issue(t, carry):
            for k in range(TOP_K):
                d = dest_ref[t * TOP_K + k]
                pltpu.make_async_copy(row_scr.at[s, pl.ds(t, 1)], o_ref.at[pl.ds(d, 1)], sem.at[s]).start()
            return carry

        lax.fori_loop(0, tt, issue, 0)

    for s in range(2):
        pl.when(slot == s)(functools.partial(scatter, s))

    @pl.when(i > 0)
    def _():
        wait_tile(1 - slot)

    @pl.when(i == tiles - 1)
    def _():
        wait_tile(slot)


def _dispatch(fill, dest_flat, g, x_p, x_s, rows, tm):
    m_p, d = x_p.shape
    m_s = x_s.shape[0]
    tt = TT_MOVE
    tiles_p = m_p // tt
    tiles = tiles_p + m_s // tt
    return pl.pallas_call(
        functools.partial(_dispatch_body, tt=tt, tiles_p=tiles_p, tiles=tiles, tm=tm, n_blocks=rows // tm),
        grid=(tiles,),
        in_specs=[pl.BlockSpec(memory_space=pltpu.SMEM),
                  pl.BlockSpec((tt * TOP_K,), lambda i: (i,), memory_space=pltpu.SMEM),
                  pl.BlockSpec((1, d), lambda i: (0, 0)),
                  pl.BlockSpec((tt, d), lambda i: (jnp.minimum(i, tiles_p - 1), 0)),
                  pl.BlockSpec((tt, d), lambda i: (jnp.maximum(i - tiles_p, 0), 0))],
        out_specs=pl.BlockSpec(memory_space=pl.ANY),
        out_shape=jax.ShapeDtypeStruct((rows, d), F32),
        scratch_shapes=[pltpu.VMEM((tm, d), F32), pltpu.VMEM((2, tt, d), F32), pltpu.SemaphoreType.DMA((2,)),
                        pltpu.SemaphoreType.DMA],
        compiler_params=pltpu.CompilerParams(dimension_semantics=("arbitrary",), vmem_limit_bytes=VMEM_LIMIT,
                                             has_side_effects=True),
        name="dispatch",
    )(fill, dest_flat, g, x_p, x_s)


def _experts_body(blk_ref, used_ref, x_ref, wg_ref, wu_ref, bg_ref, bu_ref, wd_ref, bd_ref, o_ref, h_scr, acc_scr):
    b = pl.program_id(0)
    f = pl.program_id(1)

    @pl.when(b < used_ref[0])
    def _():
        @pl.when(f == 0)
        def _():
            h_scr[...] = x_ref[...].astype(BF16)
            acc_scr[...] = jnp.zeros(acc_scr.shape, F32)

        h = h_scr[...]
        g = jnp.dot(h, wg_ref[...], preferred_element_type=F32) + bg_ref[...]
        u = jnp.dot(h, wu_ref[...], preferred_element_type=F32) + bu_ref[...]
        g = jnp.minimum(g, SWIGLU_LIMIT)
        u = jnp.clip(u, -SWIGLU_LIMIT, SWIGLU_LIMIT)
        act = (u + 1.0) * g * jax.nn.sigmoid(SWIGLU_ALPHA * g)
        acc_scr[...] += jnp.dot(act.astype(BF16), wd_ref[...], preferred_element_type=F32)

        @pl.when(f == pl.num_programs(1) - 1)
        def _():
            o_ref[...] = acc_scr[...] + bd_ref[...]

    @pl.when((b >= used_ref[0]) & (f == 0))
    def _():
        o_ref[...] = jnp.zeros(o_ref.shape, F32)


def _experts(blk_e, n_used, xs, w_gu, b_gu, w_d, b_d, tm):
    rows, d = xs.shape
    tf = TF_EXP
    nf = D_FF // tf
    n_blocks = rows // tm

    def blk(b, used):
        return jnp.minimum(b, used[0] - 1)

    def fch(b, f, used):
        return jnp.where(b < used[0], f, nf - 1)

    grid_spec = pltpu.PrefetchScalarGridSpec(
        num_scalar_prefetch=2,
        grid=(n_blocks, nf),
        in_specs=[
            pl.BlockSpec((tm, d), lambda b, f, e, u: (blk(b, u), 0)),
            pl.BlockSpec((None, d, tf), lambda b, f, e, u: (e[blk(b, u)], 0, fch(b, f, u))),
            pl.BlockSpec((None, d, tf), lambda b, f, e, u: (e[blk(b, u)], 0, nf + fch(b, f, u))),
            pl.BlockSpec((None, 1, tf), lambda b, f, e, u: (e[blk(b, u)], 0, fch(b, f, u))),
            pl.BlockSpec((None, 1, tf), lambda b, f, e, u: (e[blk(b, u)], 0, nf + fch(b, f, u))),
            pl.BlockSpec((None, tf, d), lambda b, f, e, u: (e[blk(b, u)], fch(b, f, u), 0)),
            pl.BlockSpec((None, 1, d), lambda b, f, e, u: (e[blk(b, u)], 0, 0)),
        ],
        out_specs=pl.BlockSpec((tm, d), lambda b, f, e, u: (b, 0)),
        scratch_shapes=[pltpu.VMEM((tm, d), BF16), pltpu.VMEM((tm, d), F32)],
    )
    return pl.pallas_call(
        _experts_body,
        grid_spec=grid_spec,
        out_shape=jax.ShapeDtypeStruct((rows, d), F32),
        compiler_params=_params(2),
        name="experts",
    )(blk_e, n_used, xs, w_gu, w_gu, b_gu, b_gu, w_d, b_d)


def _combine_body(dest_ref, next_ref, x_ref, gate_ref, ys_ref, o_ref, buf, sem, *, tt, tiles):
    i = pl.program_id(0)
    slot = i % 2

    def gather(d_ref, s):
        def issue(t, carry):
            for k in range(TOP_K):
                d = d_ref[t * TOP_K + k]
                pltpu.make_async_copy(ys_ref.at[pl.ds(d, 1)], buf.at[s, k, pl.ds(t, 1)], sem.at[s]).start()
            return carry

        lax.fori_loop(0, tt, issue, 0)

    @pl.when(i == 0)
    def _():
        gather(dest_ref, 0)

    for s in range(2):
        pl.when((i + 1 < tiles) & (slot == 1 - s))(functools.partial(gather, next_ref, s))

    for k in range(TOP_K):
        pltpu.make_async_copy(ys_ref.at[pl.ds(0, tt)], buf.at[slot, k], sem.at[slot]).wait()

    gate = gate_ref[...]
    out = x_ref[...]
    for k in range(TOP_K):
        out = out + gate[:, k:k + 1] * buf[slot, k]
    o_ref[...] = out


def _combine(dest_flat, x, gate, ys):
    m, d = x.shape
    tt = TT_MOVE
    tiles = m // tt
    return pl.pallas_call(
        functools.partial(_combine_body, tt=tt, tiles=tiles),
        grid=(tiles,),
        in_specs=[pl.BlockSpec((tt * TOP_K,), lambda i: (i,), memory_space=pltpu.SMEM),
                  pl.BlockSpec((tt * TOP_K,), lambda i: (jnp.minimum(i + 1, tiles - 1),), memory_space=pltpu.SMEM),
                  pl.BlockSpec((tt, d), lambda i: (i, 0)),
                  pl.BlockSpec((tt, LANES), lambda i: (i, 0)),
                  pl.BlockSpec(memory_space=pl.ANY)],
        out_specs=pl.BlockSpec((tt, d), lambda i: (i, 0)),
        out_shape=jax.ShapeDtypeStruct((m, d), F32),
        scratch_shapes=[pltpu.VMEM((2, TOP_K, tt, d), F32), pltpu.SemaphoreType.DMA((2,))],
        compiler_params=_params(1),
        name="combine",
    )(dest_flat, dest_flat, x, gate, ys)


def _moe(x_p, x_s, g_ffn, w_router, b_router, w_gate_up, b_gate_up, w_down, b_down):
    d = x_p.shape[1]
    tm = TM_EXP
    g = g_ffn.astype(F32).reshape(1, d)
    w_r = jnp.pad(w_router.astype(F32), ((0, 0), (0, LANES - N_EXPERTS)))
    w_hi = w_r.astype(BF16)
    w_lo = (w_r - w_hi.astype(F32)).astype(BF16)
    b_r = jnp.pad(b_router.astype(F32), (0, LANES - N_EXPERTS), constant_values=NEG).reshape(1, LANES)

    sel_p, gate_p, counts = _router(x_p, g, w_hi, w_lo, b_r, jnp.zeros((1, LANES), F32))
    sel_s, gate_s, counts = _router(x_s, g, w_hi, w_lo, b_r, counts)

    counts = counts[0, :N_EXPERTS].astype(I32)
    padded = (counts + tm - 1) // tm * tm
    pad_end = jnp.cumsum(padded)
    pad_start = pad_end - padded
    n_assign = (x_p.shape[0] + x_s.shape[0]) * TOP_K
    n_blocks = -(-n_assign // tm) + N_EXPERTS
    rows = n_blocks * tm
    blk_start = jnp.arange(n_blocks, dtype=I32) * tm
    blk_e = jnp.minimum(jnp.sum(pad_end[None, :] <= blk_start[:, None], axis=1), N_EXPERTS - 1).astype(I32)
    n_used = (pad_end[-1:] // tm).astype(I32)
    fill = jnp.concatenate([jnp.where(counts > 0, pad_end - tm, -1), n_used]).astype(I32)

    def dest_of(sel):
        return (pad_start[sel[:, :TOP_K]] + sel[:, TOP_K:2 * TOP_K]).reshape(-1)

    dest_p, dest_s = dest_of(sel_p), dest_of(sel_s)
    xs = _dispatch(fill, jnp.concatenate([dest_p, dest_s]), g, x_p, x_s, rows, tm)
    ys = _experts(blk_e, n_used, xs, w_gate_up, b_gate_up.astype(F32)[:, None, :],
                  w_down, b_down.astype(F32)[:, None, :], tm)
    return _combine(dest_p, x_p, gate_p, ys), _combine(dest_s, x_s, gate_s, ys)


def kernel(x_prompt, x_sample, cache_a_k, cache_a_v, cache_b_k, cache_b_v, cache_b_logf, g_attn, w_in, b_gate,
           a_q_norm, a_k_norm, b_q_norm, b_k_norm, b_f, lambda_q1, lambda_k1, lambda_q2, lambda_k2, a_subln,
           w_o_a, w_o_b, w_out, g_ffn, w_router, b_router, w_gate_up, b_gate_up, w_down, b_down):
    depth = g_attn.shape[0]
    assert depth == 1, "single-layer trunk"
    batch, seq, d = x_prompt.shape
    dec_batch, dec_seq, _ = x_sample.shape
    past = cache_a_k.shape[2]
    assert dec_seq == CHUNK and past % CHUNK == 0, "the new sample frames must form exactly one chunk"
    sk = past + dec_seq
    sk_pad = -(-sk // LANES) * LANES

    xp = x_prompt.reshape(batch * seq, d)
    xs = x_sample.reshape(dec_batch * dec_seq, d)
    g_a = g_attn[0].astype(F32)
    w_in0 = w_in[0]
    split = 2 * A_QK + A_V + 3 * B_QK + H_B
    w_ga = w_in0[:, split:split + d].astype(BF16)
    w_gb = w_in0[:, split + d:split + 2 * d].astype(BF16)
    b_ga = b_gate[0, :d].astype(F32).reshape(1, d)
    b_gb = b_gate[0, d:].astype(F32).reshape(1, d)
    w_oa, w_ob, w_o = w_o_a[0].astype(BF16), w_o_b[0].astype(BF16), w_out[0].astype(BF16)

    slopes = jnp.exp2(-8.0 * jnp.arange(1, H_A + 1, dtype=F32) / H_A)
    lam = (jnp.exp(jnp.sum(lambda_q1[0].astype(F32) * lambda_k1[0].astype(F32)))
           - jnp.exp(jnp.sum(lambda_q2[0].astype(F32) * lambda_k2[0].astype(F32))) + LAM_INIT).reshape(1)
    subln = a_subln[0].astype(F32).reshape(1, DV_A)

    norms = (a_q_norm[0], a_k_norm[0], b_q_norm[0], b_k_norm[0], b_f[0])

    hp, aq, (pak, ak16), (pav, av16), bq, (pbk, bk16), (pbv, bv16), plogf, plogf_t = _project_qkv(
        xp, g_a, w_in0, *norms)
    cp = _cumsum_lanes(plogf_t.reshape(H_B, batch, seq).transpose(1, 0, 2))
    n_e, _, two_f = w_gate_up[0].shape
    oa_p, w_gu16 = _attn_a_prompt(aq, ak16, av16, slopes, lam, subln, w_gate_up[0].reshape(n_e * d, two_f),
                                  batch, seq)
    ob_p, w_d16 = _attn_b_prompt(bq, bk16, bv16, cp, w_down[0].reshape(n_e * D_FF, d), batch, seq)
    x2_p = _merge(xp, hp, oa_p, ob_p, w_oa, w_ob, w_ga, w_gb, b_ga, b_gb, w_o)

    hs, aq, (sak, ak16), (sav, av16), bq, (sbk, bk16), (sbv, bv16), slogf, slogf_t = _project_qkv(
        xs, g_a, w_in0, *norms)
    logf_all = jnp.concatenate(
        [cache_b_logf[0].astype(F32).transpose(0, 2, 1),
         slogf_t.reshape(H_B, dec_batch, dec_seq).transpose(1, 0, 2),
         jnp.zeros((dec_batch, H_B, sk_pad - sk), F32)], axis=2)
    cs = _cumsum_lanes(logf_all)
    rows_of = lambda cache: cache[0].reshape(dec_batch, past * cache.shape[3], LANES)
    oa_s = _attn_sample(_attn_a_sample_body, aq, rows_of(cache_a_k), rows_of(cache_a_v), ak16, av16, subln,
                        (slopes, lam), dec_batch, past, dec_seq)
    ob_s = _attn_sample(_attn_b_sample_body, bq, rows_of(cache_b_k), rows_of(cache_b_v), bk16, bv16, cs, (),
                        dec_batch, past, dec_seq)
    x2_s = _merge(xs, hs, oa_s, ob_s, w_oa, w_ob, w_ga, w_gb, b_ga, b_gb, w_o)

    y_p, y_s = _moe(x2_p, x2_s, g_ffn[0], w_router[0], b_router[0], w_gu16.reshape(n_e, d, two_f), b_gate_up[0],
                    w_d16.reshape(n_e, D_FF, d), b_down[0])

    def heads(a, b, s, h):
        return a.reshape(1, b, s, h, -1)

    return (y_p.reshape(batch, seq, d), y_s.reshape(dec_batch, dec_seq, d),
            heads(pak, batch, seq, H_A), heads(pav, batch, seq, H_A), heads(pbk, batch, seq, H_B),
            heads(pbv, batch, seq, H_B), plogf.reshape(1, batch, seq, H_B),
            heads(sak, dec_batch, dec_seq, H_A), heads(sav, dec_batch, dec_seq, H_A),
            heads(sbk, dec_batch, dec_seq, H_B), heads(sbv, dec_batch, dec_seq, H_B),
            slogf.reshape(1, dec_batch, dec_seq, H_B))
```

```python
import functools
import math

import jax
import jax.numpy as jnp
from jax import lax
from jax.experimental import pallas as pl
from jax.experimental.pallas import tpu as pltpu

F32 = jnp.float32
BF16 = jnp.bfloat16
I32 = jnp.int32

D_MODEL = 2048
CHUNK = 64
H_A, DA, DV_A = 8, 64, 128
H_B, DB = 8, 128
A_QK, A_V, B_QK = H_A * 2 * DA, H_A * DV_A, H_B * DB
N_EXPERTS, TOP_K, D_FF = 32, 4, 2048
SWIGLU_LIMIT, SWIGLU_ALPHA = 7.0, 1.702
NORM_EPS = 1e-6
LAM_INIT = 0.8 - 0.6 * math.exp(-0.3 * 0)

LANES = 128
VMEM_LIMIT = 56 * 1024 * 1024
NEG = -1e30

TM_PROJ = 1024
RB_PROJ = 256
BK_ATTN = 512
SUB_ATTN = 2
TM_MERGE = 512
TN_MERGE = 512
TM_ROUTE = 512
TT_MOVE = 512
TM_EXP = 512
TF_EXP = 1024


def _params(n_axes):
    return pltpu.CompilerParams(dimension_semantics=("arbitrary",) * n_axes, vmem_limit_bytes=VMEM_LIMIT)


def _row_chunks(n_rows):
    rb = min(RB_PROJ, n_rows)
    return [slice(r0, r0 + rb) for r0 in range(0, n_rows, rb)]


def _group_rms(y, gain, group):
    lane = lax.broadcasted_iota(I32, (y.shape[0], LANES), 1)
    for s in range(y.shape[1] // LANES):
        cols = slice(s * LANES, (s + 1) * LANES)
        slab = y[:, cols]
        sq = slab * slab
        if group == LANES:
            inv = lax.rsqrt(jnp.mean(sq, axis=1, keepdims=True) + NORM_EPS)
        else:
            lo = lane < group
            s_lo = jnp.sum(jnp.where(lo, sq, 0.0), axis=1, keepdims=True)
            s_hi = jnp.sum(jnp.where(lo, 0.0, sq), axis=1, keepdims=True)
            inv = jnp.where(lo, lax.rsqrt(s_lo / group + NORM_EPS), lax.rsqrt(s_hi / group + NORM_EPS))
        yield cols, slab * inv * gain


def _proj_plain_body(h_ref, w_ref, *o_refs):
    for rows in _row_chunks(h_ref.shape[0]):
        y = jnp.dot(h_ref[rows, :], w_ref[...], preferred_element_type=F32)
        for o_ref in o_refs:
            o_ref[rows, :] = y.astype(o_ref.dtype)


def _proj_norm_body(h_ref, w_ref, g_ref, *o_refs, group, scale):
    gain = g_ref[...] * scale
    for rows in _row_chunks(h_ref.shape[0]):
        y = jnp.dot(h_ref[rows, :], w_ref[...], preferred_element_type=F32)
        for cols, val in _group_rms(y, gain, group):
            for o_ref in o_refs:
                o_ref[rows, cols] = val.astype(o_ref.dtype)


def _proj_first_body(x_ref, gx_ref, w_ref, g_ref, h_ref, o_ref, *, group, scale):
    gain = g_ref[...] * scale
    for rows in _row_chunks(x_ref.shape[0]):
        x = x_ref[rows, :]
        ms = jnp.mean(x * x, axis=-1, keepdims=True)
        h = (x * lax.rsqrt(ms + NORM_EPS) * gx_ref[...]).astype(BF16)
        h_ref[rows, :] = h
        y = jnp.dot(h, w_ref[...], preferred_element_type=F32)
        for cols, val in _group_rms(y, gain, group):
            o_ref[rows, cols] = val.astype(o_ref.dtype)


def _proj_logf_body(h_ref, w_ref, b_ref, o_ref, ot_ref):
    z = jnp.dot(h_ref[...], w_ref[...], preferred_element_type=F32) + b_ref[...]
    logf = jnp.minimum(z, 0.0) - jnp.log1p(jnp.exp(-jnp.abs(z)))
    o_ref[...] = logf[:, :H_B]
    ot_ref[...] = logf.T[:H_B, :]


def _proj(body, h, w, extra, out_shapes, out_specs, pre=()):
    m, d = h.shape
    tm = min(TM_PROJ, m)
    const = lambda a: pl.BlockSpec(a.shape, lambda i: (0, 0))
    in_specs = [pl.BlockSpec((tm, d), lambda i: (i, 0))] + [const(a) for a in (*pre, w, *extra)]
    return pl.pallas_call(
        body,
        grid=(m // tm,),
        in_specs=in_specs,
        out_specs=out_specs(tm),
        out_shape=out_shapes,
        compiler_params=_params(1),
        name="proj",
    )(h, *pre, w, *extra)


def _project_qkv(x, g_x, w_in, a_q_norm, a_k_norm, b_q_norm, b_k_norm, b_f):
    m, d = x.shape
    w = w_in.astype(BF16)
    o = 0
    cols = {}
    for name, width in (("aq", A_QK), ("ak", A_QK), ("av", A_V), ("bq", B_QK), ("bk", B_QK), ("bv", B_QK)):
        cols[name] = w[:, o:o + width]
        o += width
    w_f = jnp.pad(w[:, o:o + H_B], ((0, 0), (0, LANES - H_B)))
    b_f_pad = jnp.pad(b_f.astype(F32), (0, LANES - H_B)).reshape(1, LANES)
    tiled = lambda gain, group: jnp.tile(gain.astype(F32), LANES // group).reshape(1, LANES)

    def shapes_specs(n, dtypes):
        shapes = tuple(jax.ShapeDtypeStruct((m, n), dt) for dt in dtypes)
        specs = lambda tm: tuple(pl.BlockSpec((tm, n), lambda i: (i, 0)) for _ in dtypes)
        return shapes, specs

    def run(body, name, extra, dtypes):
        return _proj(body, h, cols[name], extra, *shapes_specs(cols[name].shape[1], dtypes))

    def normed(name, gain, group, scale, dtypes):
        return run(functools.partial(_proj_norm_body, group=group, scale=scale), name, [tiled(gain, group)], dtypes)

    first_shapes = (jax.ShapeDtypeStruct((m, d), BF16), jax.ShapeDtypeStruct((m, A_QK), BF16))
    first_specs = lambda tm: (pl.BlockSpec((tm, d), lambda i: (i, 0)), pl.BlockSpec((tm, A_QK), lambda i: (i, 0)))
    h, aq = _proj(functools.partial(_proj_first_body, group=DA, scale=DA ** -0.5), x, cols["aq"],
                  [tiled(a_q_norm, DA)], first_shapes, first_specs, pre=(g_x.astype(F32).reshape(1, d),))
    ak = normed("ak", a_k_norm, DA, 1.0, (F32, BF16))
    av = run(_proj_plain_body, "av", [], (F32, BF16))
    (bq,) = normed("bq", b_q_norm, DB, DB ** -0.5, (BF16,))
    bk = normed("bk", b_k_norm, DB, 1.0, (F32, BF16))
    bv = run(_proj_plain_body, "bv", [], (F32, BF16))
    logf, logf_t = _proj(
        _proj_logf_body, h, w_f, [b_f_pad],
        (jax.ShapeDtypeStruct((m, H_B), F32), jax.ShapeDtypeStruct((H_B, m), F32)),
        lambda tm: (pl.BlockSpec((tm, H_B), lambda i: (i, 0)), pl.BlockSpec((H_B, tm), lambda i: (0, i))))
    return h, aq, ak, av, bq, bk, bv, logf, logf_t


def _cumsum_body(x_ref, o_ref):
    x = x_ref[...]
    n = x.shape[1]
    lane = lax.broadcasted_iota(I32, x.shape, 1)
    shift = 1
    while shift < n:
        x = x + jnp.where(lane >= shift, pltpu.roll(x, shift, 1), 0.0)
        shift *= 2
    o_ref[...] = x


def _cumsum_lanes(x, batch=None):
    if batch is None:
        b, h, s = x.shape
        in_spec = pl.BlockSpec((None, h, s), lambda i: (i, 0, 0))
    else:
        b, h, s = batch, x.shape[0], x.shape[1] // batch
        in_spec = pl.BlockSpec((h, s), lambda i: (0, i))
    return pl.pallas_call(
        _cumsum_body,
        grid=(b,),
        in_specs=[in_spec],
        out_specs=pl.BlockSpec((None, h, s), lambda i: (i, 0, 0)),
        out_shape=jax.ShapeDtypeStruct((b, h, s), F32),
        compiler_params=_params(1),
        name="cumsum",
    )(x)


def _nt_dot(a, b):
    return lax.dot_general(a, b, (((1,), (1,)), ((), ())), preferred_element_type=F32)


def _with_ones(v):
    return jnp.concatenate([v, jnp.ones_like(v)], axis=1)


def _flash_sweep(q, k_ref, v_ref, s_scr, m_scr, acc_scr, qi, bk, past_bias, diag_bias):
    n_chunks = q.shape[0] // bk
    buf_a, buf_b = s_scr.at[0], s_scr.at[1]

    def scores(tile, buf, first_sub=0):
        k0 = pl.multiple_of(tile * bk, bk)
        k = k_ref[pl.ds(k0, bk), :]
        for c in range(n_chunks):
            if c % SUB_ATTN >= first_sub:
                buf[c * bk:(c + 1) * bk, :] = _nt_dot(q[c * bk:(c + 1) * bk, :], k)

    def consume(tile, buf, bias_of_sub):
        k0 = pl.multiple_of(tile * bk, bk)
        v_aug = _with_ones(v_ref[pl.ds(k0, bk), :])
        for c in range(n_chunks):
            bias = bias_of_sub(c % SUB_ATTN)
            if bias is None:
                continue
            rows = slice(c * bk, (c + 1) * bk)
            s = buf[rows, :] + bias
            m_prev = m_scr[rows, :]
            m_new = jnp.maximum(m_prev, jnp.max(s, axis=1, keepdims=True))
            alpha = jnp.exp(m_prev - m_new)
            p = jnp.exp(s - jnp.tile(m_new, (1, bk // LANES)))
            acc_scr[rows, :] = (jnp.tile(alpha, (1, 2)) * acc_scr[rows, :]
                                + jnp.dot(p.astype(BF16), v_aug, preferred_element_type=F32))
            m_scr[rows, :] = m_new

    scores(0, buf_a)

    def tile_pair(t, carry):
        j = 2 * t
        scores(j + 1, buf_b)
        consume(j, buf_a, lambda sub: past_bias(j))
        scores(j + 2, buf_a)
        consume(j + 1, buf_b, lambda sub: past_bias(j + 1))
        return carry

    assert SUB_ATTN == 2, "the pair loop and the two-tile tail below are written for two key tiles per query tile"
    lax.fori_loop(0, qi, tile_pair, 0)
    first = SUB_ATTN * qi
    scores(first + 1, buf_b, first_sub=1)
    consume(first, buf_a, lambda sub: diag_bias(0) if sub == 0 else past_bias(first))
    consume(first + 1, buf_b, lambda sub: diag_bias(1) if sub == 1 else None)


def _init_softmax(m_scr, acc_scr):
    m_scr[...] = jnp.full(m_scr.shape, NEG, F32)
    acc_scr[...] = jnp.zeros(acc_scr.shape, F32)


def _stack_components(q):
    lane = lax.broadcasted_iota(I32, q.shape, 1)
    zero = jnp.zeros_like(q)
    return jnp.concatenate([jnp.where(lane < DA, q, zero), jnp.where(lane >= DA, q, zero)], axis=0)


def _diff_combine(o1, o2, lam, gain):
    o = o1 - lam * o2
    inv = lax.rsqrt(jnp.mean(o * o, axis=1, keepdims=True) + NORM_EPS)
    return o * inv * gain * (1.0 - LAM_INIT)


def _first_step():
    return (pl.program_id(0) == 0) & (pl.program_id(1) == 0) & (pl.program_id(2) == 0)


def _attn_a_prompt_body(slope_ref, lam_ref, q_ref, k_ref, v_ref, g_ref, wi_ref, o_ref, wo_ref,
                        q_scr, rel_scr, s_scr, m_scr, acc_scr, *, bk):
    h = pl.program_id(1)
    qi = pl.program_id(2)
    bq = SUB_ATTN * bk
    slope = slope_ref[h]

    @pl.when(_first_step())
    def _():
        r = lax.broadcasted_iota(I32, (bk, bk), 0)
        c = lax.broadcasted_iota(I32, (bk, bk), 1)
        rel_scr[...] = jnp.where(c // CHUNK <= r // CHUNK, (r - jnp.abs(r - c)).astype(F32), NEG)

    wo_ref[...] = wi_ref[...].astype(wo_ref.dtype)
    q_scr[...] = _stack_components(q_ref[...])
    _init_softmax(m_scr, acc_scr)
    col = lax.broadcasted_iota(I32, (1, bk), 1).astype(F32)
    _flash_sweep(q_scr, k_ref, v_ref, s_scr, m_scr, acc_scr, qi, bk,
                 past_bias=lambda j: slope * (col + (j * bk - qi * bq).astype(F32)),
                 diag_bias=lambda sub: slope * (rel_scr[...] + float(sub * bk)))
    acc = acc_scr[...]
    o1 = acc[:bq, :LANES] / acc[:bq, LANES:]
    o2 = acc[bq:, :LANES] / acc[bq:, LANES:]
    o_ref[...] = _diff_combine(o1, o2, lam_ref[0], g_ref[...]).astype(o_ref.dtype)


def _cast_specs(w, steps, index):
    rows = w.shape[0] // steps
    assert rows * steps == w.shape[0] and rows % 16 == 0
    spec = pl.BlockSpec((rows, w.shape[1]), lambda b, h, i: (index(b, h, i), 0))
    return spec, spec, jax.ShapeDtypeStruct(w.shape, BF16)


def _attn_a_prompt(aq, ak, av, slopes, lam, gain, w_cast, batch, seq):
    bk = min(BK_ATTN, seq // SUB_ATTN)
    bq = SUB_ATTN * bk
    nq = seq // bq
    smem = pl.BlockSpec(memory_space=pltpu.SMEM)
    w_in_spec, w_out_spec, w_shape = _cast_specs(w_cast, batch * H_A * nq, lambda b, h, i: (b * H_A + h) * nq + i)
    return pl.pallas_call(
        functools.partial(_attn_a_prompt_body, bk=bk),
        grid=(batch, H_A, nq),
        in_specs=[smem, smem,
                  pl.BlockSpec((bq, LANES), lambda b, h, i: (b * nq + i, h)),
                  pl.BlockSpec((seq, LANES), lambda b, h, i: (b, h)),
                  pl.BlockSpec((seq, LANES), lambda b, h, i: (b, h)),
                  pl.BlockSpec((1, LANES), lambda b, h, i: (0, 0)),
                  w_in_spec],
        out_specs=(pl.BlockSpec((bq, LANES), lambda b, h, i: (b * nq + i, h)), w_out_spec),
        out_shape=(jax.ShapeDtypeStruct(aq.shape, BF16), w_shape),
        scratch_shapes=[pltpu.VMEM((2 * bq, LANES), BF16), pltpu.VMEM((bk, bk), F32),
                        pltpu.VMEM((2, 2 * bq, bk), F32), pltpu.VMEM((2 * bq, LANES), F32),
                        pltpu.VMEM((2 * bq, 2 * LANES), F32)],
        compiler_params=_params(3),
        name="attn_a_prompt",
    )(slopes, lam, aq, ak, av, gain, w_cast)


def _attn_b_prompt_body(q_ref, k_ref, v_ref, c_ref, wi_ref, o_ref, wo_ref, mask_scr, s_scr, m_scr, acc_scr, *, bk):
    qi = pl.program_id(2)

    @pl.when(_first_step())
    def _():
        r = lax.broadcasted_iota(I32, (bk, bk), 0)
        c = lax.broadcasted_iota(I32, (bk, bk), 1)
        mask_scr[...] = jnp.where(c <= r, 0.0, NEG)

    wo_ref[...] = wi_ref[...].astype(wo_ref.dtype)
    _init_softmax(m_scr, acc_scr)
    first = SUB_ATTN * qi
    c_first = c_ref[pl.ds(first, 1), :][:, 0:1]
    past_bias = lambda j: c_first - c_ref[pl.ds(j, 1), :]
    _flash_sweep(q_ref, k_ref, v_ref, s_scr, m_scr, acc_scr, qi, bk,
                 past_bias=past_bias, diag_bias=lambda sub: mask_scr[...] + past_bias(first + sub))
    acc = acc_scr[...]
    o_ref[...] = (acc[:, :LANES] / acc[:, LANES:]).astype(o_ref.dtype)


def _attn_b_prompt(bq_arr, k_arr, v_arr, c, w_cast, batch, seq):
    bk = min(BK_ATTN, seq // SUB_ATTN)
    bq = SUB_ATTN * bk
    nq = seq // bq
    c4 = c.reshape(batch, H_B, seq // bk, bk)
    w_in_spec, w_out_spec, w_shape = _cast_specs(w_cast, batch * H_B * nq, lambda b, h, i: (b * H_B + h) * nq + i)
    return pl.pallas_call(
        functools.partial(_attn_b_prompt_body, bk=bk),
        grid=(batch, H_B, nq),
        in_specs=[pl.BlockSpec((bq, LANES), lambda b, h, i: (b * nq + i, h)),
                  pl.BlockSpec((seq, LANES), lambda b, h, i: (b, h)),
                  pl.BlockSpec((seq, LANES), lambda b, h, i: (b, h)),
                  pl.BlockSpec((None, None, seq // bk, bk), lambda b, h, i: (b, h, 0, 0)),
                  w_in_spec],
        out_specs=(pl.BlockSpec((bq, LANES), lambda b, h, i: (b * nq + i, h)), w_out_spec),
        out_shape=(jax.ShapeDtypeStruct(bq_arr.shape, BF16), w_shape),
        scratch_shapes=[pltpu.VMEM((bk, bk), F32), pltpu.VMEM((2, bq, bk), F32), pltpu.VMEM((bq, LANES), F32),
                        pltpu.VMEM((bq, 2 * LANES), F32)],
        compiler_params=_params(3),
        name="attn_b_prompt",
    )(bq_arr, k_arr, v_arr, c4, w_cast)


def _two_part_softmax(s_c, s_n, v_c, v_n):
    m = jnp.maximum(jnp.max(s_c, axis=1, keepdims=True), jnp.max(s_n, axis=1, keepdims=True))
    p_c = jnp.exp(s_c - m)
    p_n = jnp.exp(s_n - m)
    l = jnp.sum(p_c, axis=1, keepdims=True) + jnp.sum(p_n, axis=1, keepdims=True)
    acc = (jnp.dot(p_c.astype(BF16), v_c, preferred_element_type=F32)
           + jnp.dot(p_n.astype(BF16), v_n, preferred_element_type=F32))
    return acc / l


def _attn_a_sample_body(slope_ref, lam_ref, q_ref, kc_ref, vc_ref, kn_ref, vn_ref, g_ref, o_ref, *, past, sq):
    col_c = lax.broadcasted_iota(I32, (1, past), 1).astype(F32) - float(past)
    r = lax.broadcasted_iota(I32, (2 * sq, sq), 0)
    r = jnp.where(r >= sq, r - sq, r)
    c = lax.broadcasted_iota(I32, (2 * sq, sq), 1)
    rel_n = (r - jnp.abs(r - c)).astype(F32)
    for h in range(H_A):
        hs = slice(h * LANES, (h + 1) * LANES)
        slope = slope_ref[h]
        qq = _stack_components(q_ref[:, hs])
        s_c = _nt_dot(qq, kc_ref[pl.ds(h, past, stride=H_A), :].astype(BF16)) + slope * col_c
        s_n = _nt_dot(qq, kn_ref[:, hs]) + slope * rel_n
        o = _two_part_softmax(s_c, s_n, vc_ref[pl.ds(h, past, stride=H_A), :].astype(BF16), vn_ref[:, hs])
        o_ref[:, hs] = _diff_combine(o[:sq], o[sq:], lam_ref[0], g_ref[...]).astype(o_ref.dtype)


def _attn_b_sample_body(q_ref, kc_ref, vc_ref, kn_ref, vn_ref, c_ref, o_ref, *, past, sq):
    r = lax.broadcasted_iota(I32, (sq, sq), 0)
    c = lax.broadcasted_iota(I32, (sq, sq), 1)
    causal = c <= r
    for h in range(H_B):
        hs = slice(h * LANES, (h + 1) * LANES)
        q = q_ref[:, hs]
        c_row = c_ref[h:h + 1, :]
        c_first = c_row[:, past:past + 1]
        s_c = _nt_dot(q, kc_ref[pl.ds(h, past, stride=H_B), :].astype(BF16)) + (c_first - c_row[:, :past])
        s_n = _nt_dot(q, kn_ref[:, hs]) + (c_first - c_row[:, past:past + sq])
        s_n = jnp.where(causal, s_n, NEG)
        o = _two_part_softmax(s_c, s_n, vc_ref[pl.ds(h, past, stride=H_B), :].astype(BF16), vn_ref[:, hs])
        o_ref[:, hs] = o.astype(o_ref.dtype)


def _attn_sample(body, q, kc, vc, kn, vn, tail, smem_args, batch, past, sq):
    width = q.shape[1]
    smem = pl.BlockSpec(memory_space=pltpu.SMEM)
    cache = pl.BlockSpec((None,) + kc.shape[1:], lambda b: (b, 0, 0))
    in_specs = [smem] * len(smem_args) + [
        pl.BlockSpec((sq, width), lambda b: (b, 0)), cache, cache,
        pl.BlockSpec((sq, width), lambda b: (b, 0)),
        pl.BlockSpec((sq, width), lambda b: (b, 0))]
    if tail.ndim == 3:
        in_specs.append(pl.BlockSpec((None,) + tail.shape[1:], lambda b: (b, 0, 0)))
    else:
        in_specs.append(pl.BlockSpec(tail.shape, lambda b: (0, 0)))
    return pl.pallas_call(
        functools.partial(body, past=past, sq=sq),
        grid=(batch,),
        in_specs=in_specs,
        out_specs=pl.BlockSpec((sq, width), lambda b: (b, 0)),
        out_shape=jax.ShapeDtypeStruct(q.shape, BF16),
        compiler_params=_params(1),
        name="attn_sample",
    )(*smem_args, q, kc, vc, kn, vn, tail)


def _merge_body(x_ref, h_ref, oa_ref, ob_ref, woa_ref, wob_ref, wga_ref, wgb_ref, bga_ref, bgb_ref, wout_ref,
                o_ref, acc_scr):
    j = pl.program_id(1)

    @pl.when(j == 0)
    def _():
        acc_scr[...] = jnp.zeros(acc_scr.shape, F32)

    h = h_ref[...]
    y_a = jnp.dot(oa_ref[...], woa_ref[...], preferred_element_type=F32)
    y_b = jnp.dot(ob_ref[...], wob_ref[...], preferred_element_type=F32)
    g_a = jax.nn.sigmoid(jnp.dot(h, wga_ref[...], preferred_element_type=F32) + bga_ref[...])
    g_b = jax.nn.sigmoid(jnp.dot(h, wgb_ref[...], preferred_element_type=F32) + bgb_ref[...])
    merged = (g_a * y_a + g_b * y_b).astype(BF16)
    acc_scr[...] += jnp.dot(merged, wout_ref[...], preferred_element_type=F32)

    @pl.when(j == pl.num_programs(1) - 1)
    def _():
        o_ref[...] = x_ref[...] + acc_scr[...]


def _merge(x, h, o_a, o_b, w_o_a, w_o_b, w_ga, w_gb, b_ga, b_gb, w_out):
    m, d = x.shape
    tm = min(TM_MERGE, m)
    tn = TN_MERGE
    row = lambda n: pl.BlockSpec((tm, n), lambda i, j: (i, 0))
    colw = lambda k: pl.BlockSpec((k, tn), lambda i, j: (0, j))
    return pl.pallas_call(
        _merge_body,
        grid=(m // tm, d // tn),
        in_specs=[row(d), row(d), row(A_V), row(B_QK), colw(A_V), colw(B_QK), colw(d), colw(d),
                  pl.BlockSpec((1, tn), lambda i, j: (0, j)), pl.BlockSpec((1, tn), lambda i, j: (0, j)),
                  pl.BlockSpec((tn, d), lambda i, j: (j, 0))],
        out_specs=row(d),
        out_shape=jax.ShapeDtypeStruct((m, d), F32),
        scratch_shapes=[pltpu.VMEM((tm, d), F32)],
        compiler_params=_params(2),
        name="merge",
    )(x, h, o_a, o_b, w_o_a, w_o_b, w_ga, w_gb, b_ga, b_gb, w_out)


def _router_body(x_ref, g_ref, whi_ref, wlo_ref, b_ref, cin_ref, sel_ref, gate_ref, cout_ref, cnt_scr):
    i = pl.program_id(0)

    @pl.when(i == 0)
    def _():
        cnt_scr[...] = cin_ref[...]

    x = x_ref[...]
    tm = x.shape[0]
    h = x * lax.rsqrt(jnp.mean(x * x, axis=-1, keepdims=True) + NORM_EPS) * g_ref[...]
    hi = h.astype(BF16)
    lo = (h - hi.astype(F32)).astype(BF16)
    logits = (jnp.dot(hi, whi_ref[...], preferred_element_type=F32)
              + jnp.dot(hi, wlo_ref[...], preferred_element_type=F32)
              + jnp.dot(lo, whi_ref[...], preferred_element_type=F32)) + b_ref[...]

    lane = lax.broadcasted_iota(I32, logits.shape, 1)
    work = logits
    picks, vals, ids = [], [], []
    for _ in range(TOP_K):
        mx = jnp.max(work, axis=1, keepdims=True)
        idx = jnp.min(jnp.where(work == mx, lane, LANES), axis=1, keepdims=True)
        pick = lane == idx
        work = jnp.where(pick, -jnp.inf, work)
        picks.append(pick)
        vals.append(mx)
        ids.append(idx)

    exps = [jnp.exp(v - vals[0]) for v in vals]
    denom = exps[0] + exps[1] + exps[2] + exps[3]

    picked = jnp.zeros(logits.shape, F32)
    for pick in picks:
        picked = picked + jnp.where(pick, 1.0, 0.0)
    r = lax.broadcasted_iota(I32, (tm, tm), 0)
    c = lax.broadcasted_iota(I32, (tm, tm), 1)
    earlier = jnp.where(c < r, 1.0, 0.0).astype(BF16)
    before = jnp.dot(earlier, picked.astype(BF16), preferred_element_type=F32) + cnt_scr[...]

    sel = jnp.zeros(logits.shape, I32)
    gate = jnp.zeros(logits.shape, F32)
    for k in range(TOP_K):
        rank = jnp.sum(jnp.where(picks[k], before, 0.0), axis=1, keepdims=True).astype(I32)
        sel = jnp.where(lane == k, ids[k], sel)
        sel = jnp.where(lane == TOP_K + k, rank, sel)
        gate = jnp.where(lane == k, exps[k] / denom, gate)
    sel_ref[...] = sel
    gate_ref[...] = gate
    cnt_scr[...] += jnp.sum(picked, axis=0, keepdims=True)
    cout_ref[...] = cnt_scr[...]


def _router(x, g, w_hi, w_lo, b, counts_in):
    m, d = x.shape
    tm = min(TM_ROUTE, m)
    const = lambda shape: pl.BlockSpec(shape, lambda i: (0, 0))
    return pl.pallas_call(
        _router_body,
        grid=(m // tm,),
        in_specs=[pl.BlockSpec((tm, d), lambda i: (i, 0)), const((1, d)), const((d, LANES)), const((d, LANES)),
                  const((1, LANES)), const((1, LANES))],
        out_specs=(pl.BlockSpec((tm, LANES), lambda i: (i, 0)), pl.BlockSpec((tm, LANES), lambda i: (i, 0)),
                   const((1, LANES))),
        out_shape=(jax.ShapeDtypeStruct((m, LANES), I32), jax.ShapeDtypeStruct((m, LANES), F32),
                   jax.ShapeDtypeStruct((1, LANES), F32)),
        scratch_shapes=[pltpu.VMEM((1, LANES), F32)],
        compiler_params=_params(1),
        name="router",
    )(x, g, w_hi, w_lo, b, counts_in)


def _dispatch_body(fill_ref, dest_ref, g_ref, xp_ref, xs_ref, o_ref, zero_scr, row_scr, sem, fill_sem, *, tt,
                   tiles_p, tiles, tm, n_blocks):
    i = pl.program_id(0)
    slot = i % 2

    @pl.when(i == 0)
    def _():
        zero_scr[...] = jnp.zeros(zero_scr.shape, F32)
        for e in range(N_EXPERTS):
            @pl.when(fill_ref[e] >= 0)
            def _():
                off = pl.multiple_of(fill_ref[e], tm)
                pltpu.make_async_copy(zero_scr, o_ref.at[pl.ds(off, tm)], fill_sem).start()
        for e in range(N_EXPERTS):
            @pl.when(fill_ref[e] >= 0)
            def _():
                pltpu.make_async_copy(zero_scr, o_ref.at[pl.ds(0, tm)], fill_sem).wait()

        def zero_block(b, carry):
            off = pl.multiple_of(b * tm, tm)
            pltpu.make_async_copy(zero_scr, o_ref.at[pl.ds(off, tm)], fill_sem).start()
            return carry

        def wait_block(b, carry):
            pltpu.make_async_copy(zero_scr, o_ref.at[pl.ds(0, tm)], fill_sem).wait()
            return carry

        lax.fori_loop(fill_ref[N_EXPERTS], n_blocks, zero_block, 0)
        lax.fori_loop(fill_ref[N_EXPERTS], n_blocks, wait_block, 0)

    def stage(src_ref):
        x = src_ref[...]
        inv = lax.rsqrt(jnp.mean(x * x, axis=-1, keepdims=True) + NORM_EPS)
        row_scr[slot] = x * inv * g_ref[...]

    @pl.when(i < tiles_p)
    def _():
        stage(xp_ref)

    @pl.when(i >= tiles_p)
    def _():
        stage(xs_ref)

    def wait_tile(s):
        for k in range(TOP_K):
            pltpu.make_async_copy(row_scr.at[s], o_ref.at[pl.ds(0, tt)], sem.at[s]).wait()

    def scatter(s):
        def issue(t, carry):
            for k in range(TOP_K):
                d = dest_ref[t * TOP_K + k]
                pltpu.make_async_copy(row_scr.at[s, pl.ds(t, 1)], o_ref.at[pl.ds(d, 1)], sem.at[s]).start()
            return carry

        lax.fori_loop(0, tt, issue, 0)

    for s in range(2):
        pl.when(slot == s)(functools.partial(scatter, s))

    @pl.when(i > 0)
    def _():
        wait_tile(1 - slot)

    @pl.when(i == tiles - 1)
    def _():
        wait_tile(slot)


def _dispatch(fill, dest_flat, g, x_p, x_s, rows, tm):
    m_p, d = x_p.shape
    m_s = x_s.shape[0]
    tt = TT_MOVE
    tiles_p = m_p // tt
    tiles = tiles_p + m_s // tt
    return pl.pallas_call(
        functools.partial(_dispatch_body, tt=tt, tiles_p=tiles_p, tiles=tiles, tm=tm, n_blocks=rows // tm),
        grid=(tiles,),
        in_specs=[pl.BlockSpec(memory_space=pltpu.SMEM),
                  pl.BlockSpec((tt * TOP_K,), lambda i: (i,), memory_space=pltpu.SMEM),
                  pl.BlockSpec((1, d), lambda i: (0, 0)),
                  pl.BlockSpec((tt, d), lambda i: (jnp.minimum(i, tiles_p - 1), 0)),
                  pl.BlockSpec((tt, d), lambda i: (jnp.maximum(i - tiles_p, 0), 0))],
        out_specs=pl.BlockSpec(memory_space=pl.ANY),
        out_shape=jax.ShapeDtypeStruct((rows, d), F32),
        scratch_shapes=[pltpu.VMEM((tm, d), F32), pltpu.VMEM((2, tt, d), F32), pltpu.SemaphoreType.DMA((2,)),
                        pltpu.SemaphoreType.DMA],
        compiler_params=pltpu.CompilerParams(dimension_semantics=("arbitrary",), vmem_limit_bytes=VMEM_LIMIT,
                                             has_side_effects=True),
        name="dispatch",
    )(fill, dest_flat, g, x_p, x_s)


def _experts_body(blk_ref, used_ref, x_ref, wg_ref, wu_ref, bg_ref, bu_ref, wd_ref, bd_ref, o_ref, h_scr, acc_scr):
    b = pl.program_id(0)
    f = pl.program_id(1)

    @pl.when(b < used_ref[0])
    def _():
        @pl.when(f == 0)
        def _():
            h_scr[...] = x_ref[...].astype(BF16)
            acc_scr[...] = jnp.zeros(acc_scr.shape, F32)

        h = h_scr[...]
        g = jnp.dot(h, wg_ref[...], preferred_element_type=F32) + bg_ref[...]
        u = jnp.dot(h, wu_ref[...], preferred_element_type=F32) + bu_ref[...]
        g = jnp.minimum(g, SWIGLU_LIMIT)
        u = jnp.clip(u, -SWIGLU_LIMIT, SWIGLU_LIMIT)
        act = (u + 1.0) * g * jax.nn.sigmoid(SWIGLU_ALPHA * g)
        acc_scr[...] += jnp.dot(act.astype(BF16), wd_ref[...], preferred_element_type=F32)

        @pl.when(f == pl.num_programs(1) - 1)
        def _():
            o_ref[...] = acc_scr[...] + bd_ref[...]

    @pl.when((b >= used_ref[0]) & (f == 0))
    def _():
        o_ref[...] = jnp.zeros(o_ref.shape, F32)


def _experts(blk_e, n_used, xs, w_gu, b_gu, w_d, b_d, tm):
    rows, d = xs.shape
    tf = TF_EXP
    nf = D_FF // tf
    n_blocks = rows // tm

    def blk(b, used):
        return jnp.minimum(b, used[0] - 1)

    def fch(b, f, used):
        return jnp.where(b < used[0], f, nf - 1)

    grid_spec = pltpu.PrefetchScalarGridSpec(
        num_scalar_prefetch=2,
        grid=(n_blocks, nf),
        in_specs=[
            pl.BlockSpec((tm, d), lambda b, f, e, u: (blk(b, u), 0)),
            pl.BlockSpec((None, d, tf), lambda b, f, e, u: (e[blk(b, u)], 0, fch(b, f, u))),
            pl.BlockSpec((None, d, tf), lambda b, f, e, u: (e[blk(b, u)], 0, nf + fch(b, f, u))),
            pl.BlockSpec((None, 1, tf), lambda b, f, e, u: (e[blk(b, u)], 0, fch(b, f, u))),
            pl.BlockSpec((None, 1, tf), lambda b, f, e, u: (e[blk(b, u)], 0, nf + fch(b, f, u))),
            pl.BlockSpec((None, tf, d), lambda b, f, e, u: (e[blk(b, u)], fch(b, f, u), 0)),
            pl.BlockSpec((None, 1, d), lambda b, f, e, u: (e[blk(b, u)], 0, 0)),
        ],
        out_specs=pl.BlockSpec((tm, d), lambda b, f, e, u: (b, 0)),
        scratch_shapes=[pltpu.VMEM((tm, d), BF16), pltpu.VMEM((tm, d), F32)],
    )
    return pl.pallas_call(
        _experts_body,
        grid_spec=grid_spec,
        out_shape=jax.ShapeDtypeStruct((rows, d), F32),
        compiler_params=_params(2),
        name="experts",
    )(blk_e, n_used, xs, w_gu, w_gu, b_gu, b_gu, w_d, b_d)


def _combine_body(dest_ref, next_ref, x_ref, gate_ref, ys_ref, o_ref, buf, sem, *, tt, tiles):
    i = pl.program_id(0)
    slot = i % 2

    def gather(d_ref, s):
        def issue(t, carry):
            for k in range(TOP_K):
                d = d_ref[t * TOP_K + k]
                pltpu.make_async_copy(ys_ref.at[pl.ds(d, 1)], buf.at[s, k, pl.ds(t, 1)], sem.at[s]).start()
            return carry

        lax.fori_loop(0, tt, issue, 0)

    @pl.when(i == 0)
    def _():
        gather(dest_ref, 0)

    for s in range(2):
        pl.when((i + 1 < tiles) & (slot == 1 - s))(functools.partial(gather, next_ref, s))

    for k in range(TOP_K):
        pltpu.make_async_copy(ys_ref.at[pl.ds(0, tt)], buf.at[slot, k], sem.at[slot]).wait()

    gate = gate_ref[...]
    out = x_ref[...]
    for k in range(TOP_K):
        out = out + gate[:, k:k + 1] * buf[slot, k]
    o_ref[...] = out


def _combine(dest_flat, x, gate, ys):
    m, d = x.shape
    tt = TT_MOVE
    tiles = m // tt
    return pl.pallas_call(
        functools.partial(_combine_body, tt=tt, tiles=tiles),
        grid=(tiles,),
        in_specs=[pl.BlockSpec((tt * TOP_K,), lambda i: (i,), memory_space=pltpu.SMEM),
                  pl.BlockSpec((tt * TOP_K,), lambda i: (jnp.minimum(i + 1, tiles - 1),), memory_space=pltpu.SMEM),
                  pl.BlockSpec((tt, d), lambda i: (i, 0)),
                  pl.BlockSpec((tt, LANES), lambda i: (i, 0)),
                  pl.BlockSpec(memory_space=pl.ANY)],
        out_specs=pl.BlockSpec((tt, d), lambda i: (i, 0)),
        out_shape=jax.ShapeDtypeStruct((m, d), F32),
        scratch_shapes=[pltpu.VMEM((2, TOP_K, tt, d), F32), pltpu.SemaphoreType.DMA((2,))],
        compiler_params=_params(1),
        name="combine",
    )(dest_flat, dest_flat, x, gate, ys)


def _moe(x_p, x_s, g_ffn, w_router, b_router, w_gate_up, b_gate_up, w_down, b_down):
    d = x_p.shape[1]
    tm = TM_EXP
    g = g_ffn.astype(F32).reshape(1, d)
    w_r = jnp.pad(w_router.astype(F32), ((0, 0), (0, LANES - N_EXPERTS)))
    w_hi = w_r.astype(BF16)
    w_lo = (w_r - w_hi.astype(F32)).astype(BF16)
    b_r = jnp.pad(b_router.astype(F32), (0, LANES - N_EXPERTS), constant_values=NEG).reshape(1, LANES)

    sel_p, gate_p, counts = _router(x_p, g, w_hi, w_lo, b_r, jnp.zeros((1, LANES), F32))
    sel_s, gate_s, counts = _router(x_s, g, w_hi, w_lo, b_r, counts)

    counts = counts[0, :N_EXPERTS].astype(I32)
    padded = (counts + tm - 1) // tm * tm
    pad_end = jnp.cumsum(padded)
    pad_start = pad_end - padded
    n_assign = (x_p.shape[0] + x_s.shape[0]) * TOP_K
    n_blocks = -(-n_assign // tm) + N_EXPERTS
    rows = n_blocks * tm
    blk_start = jnp.arange(n_blocks, dtype=I32) * tm
    blk_e = jnp.minimum(jnp.sum(pad_end[None, :] <= blk_start[:, None], axis=1), N_EXPERTS - 1).astype(I32)
    n_used = (pad_end[-1:] // tm).astype(I32)
    fill = jnp.concatenate([jnp.where(counts > 0, pad_end - tm, -1), n_used]).astype(I32)

    def dest_of(sel):
        return (pad_start[sel[:, :TOP_K]] + sel[:, TOP_K:2 * TOP_K]).reshape(-1)

    dest_p, dest_s = dest_of(sel_p), dest_of(sel_s)
    xs = _dispatch(fill, jnp.concatenate([dest_p, dest_s]), g, x_p, x_s, rows, tm)
    ys = _experts(blk_e, n_used, xs, w_gate_up, b_gate_up.astype(F32)[:, None, :],
                  w_down, b_down.astype(F32)[:, None, :], tm)
    return _combine(dest_p, x_p, gate_p, ys), _combine(dest_s, x_s, gate_s, ys)


def kernel(x_prompt, x_sample, cache_a_k, cache_a_v, cache_b_k, cache_b_v, cache_b_logf, g_attn, w_in, b_gate,
           a_q_norm, a_k_norm, b_q_norm, b_k_norm, b_f, lambda_q1, lambda_k1, lambda_q2, lambda_k2, a_subln,
           w_o_a, w_o_b, w_out, g_ffn, w_router, b_router, w_gate_up, b_gate_up, w_down, b_down):
    depth = g_attn.shape[0]
    assert depth == 1, "single-layer trunk"
    batch, seq, d = x_prompt.shape
    dec_batch, dec_seq, _ = x_sample.shape
    past = cache_a_k.shape[2]
    assert dec_seq == CHUNK and past % CHUNK == 0, "the new sample frames must form exactly one chunk"
    sk = past + dec_seq
    sk_pad = -(-sk // LANES) * LANES

    xp = x_prompt.reshape(batch * seq, d)
    xs = x_sample.reshape(dec_batch * dec_seq, d)
    g_a = g_attn[0].astype(F32)
    w_in0 = w_in[0]
    split = 2 * A_QK + A_V + 3 * B_QK + H_B
    w_ga = w_in0[:, split:split + d].astype(BF16)
    w_gb = w_in0[:, split + d:split + 2 * d].astype(BF16)
    b_ga = b_gate[0, :d].astype(F32).reshape(1, d)
    b_gb = b_gate[0, d:].astype(F32).reshape(1, d)
    w_oa, w_ob, w_o = w_o_a[0].astype(BF16), w_o_b[0].astype(BF16), w_out[0].astype(BF16)

    slopes = jnp.exp2(-8.0 * jnp.arange(1, H_A + 1, dtype=F32) / H_A)
    lam = (jnp.exp(jnp.sum(lambda_q1[0].astype(F32) * lambda_k1[0].astype(F32)))
           - jnp.exp(jnp.sum(lambda_q2[0].astype(F32) * lambda_k2[0].astype(F32))) + LAM_INIT).reshape(1)
    subln = a_subln[0].astype(F32).reshape(1, DV_A)

    norms = (a_q_norm[0], a_k_norm[0], b_q_norm[0], b_k_norm[0], b_f[0])

    hp, aq, (pak, ak16), (pav, av16), bq, (pbk, bk16), (pbv, bv16), plogf, plogf_t = _project_qkv(
        xp, g_a, w_in0, *norms)
    cp = _cumsum_lanes(plogf_t, batch=batch)
    n_e, _, two_f = w_gate_up[0].shape
    oa_p, w_gu16 = _attn_a_prompt(aq, ak16, av16, slopes, lam, subln, w_gate_up[0].reshape(n_e * d, two_f),
                                  batch, seq)
    ob_p, w_d16 = _attn_b_prompt(bq, bk16, bv16, cp, w_down[0].reshape(n_e * D_FF, d), batch, seq)
    x2_p = _merge(xp, hp, oa_p, ob_p, w_oa, w_ob, w_ga, w_gb, b_ga, b_gb, w_o)

    hs, aq, (sak, ak16), (sav, av16), bq, (sbk, bk16), (sbv, bv16), slogf, slogf_t = _project_qkv(
        xs, g_a, w_in0, *norms)
    logf_all = jnp.concatenate(
        [cache_b_logf[0].astype(F32).transpose(0, 2, 1),
         slogf_t.reshape(H_B, dec_batch, dec_seq).transpose(1, 0, 2),
         jnp.zeros((dec_batch, H_B, sk_pad - sk), F32)], axis=2)
    cs = _cumsum_lanes(logf_all)
    rows_of = lambda cache: cache[0].reshape(dec_batch, past * cache.shape[3], LANES)
    oa_s = _attn_sample(_attn_a_sample_body, aq, rows_of(cache_a_k), rows_of(cache_a_v), ak16, av16, subln,
                        (slopes, lam), dec_batch, past, dec_seq)
    ob_s = _attn_sample(_attn_b_sample_body, bq, rows_of(cache_b_k), rows_of(cache_b_v), bk16, bv16, cs, (),
                        dec_batch, past, dec_seq)
    x2_s = _merge(xs, hs, oa_s, ob_s, w_oa, w_ob, w_ga, w_gb, b_ga, b_gb, w_o)

    y_p, y_s = _moe(x2_p, x2_s, g_ffn[0], w_router[0], b_router[0], w_gu16.reshape(n_e, d, two_f), b_gate_up[0],
                    w_d16.reshape(n_e, D_FF, d), b_down[0])

    def heads(a, b, s, h):
        return a.reshape(1, b, s, h, -1)

    return (y_p.reshape(batch, seq, d), y_s.reshape(dec_batch, dec_seq, d),
            heads(pak, batch, seq, H_A), heads(pav, batch, seq, H_A), heads(pbk, batch, seq, H_B),
            heads(pbv, batch, seq, H_B), plogf.reshape(1, batch, seq, H_B),
            heads(sak, dec_batch, dec_seq, H_A), heads(sav, dec_batch, dec_seq, H_A),
            heads(sbk, dec_batch, dec_seq, H_B), heads(sbv, dec_batch, dec_seq, H_B),
            slogf.reshape(1, dec_batch, dec_seq, H_B))
```
